```python
import math
import jax, jax.numpy as jnp
from jax import lax
import numpy as np

D_MODEL = 1024
BATCH = 2
SEQ = 8192
DEPTH = 4
DEC_BATCH = 128
DEC_SEQ = 1
PAST_LEN = 8192
PAGE_SIZE = 128

N_MIXERS = 2
N_HEADS = 16
N_KV_HEADS = 4
HEAD_DIM = D_MODEL // N_HEADS
GROUP = N_HEADS // N_KV_HEADS
BRANCH = N_HEADS * HEAD_DIM
KV_WIDTH = N_KV_HEADS * HEAD_DIM
WINDOW = 128
Q_BLOCK = 128
N_SWA = (DEPTH + 1) // 2
N_FOX = DEPTH // 2
SWA_IN = 2 * BRANCH + 2 * KV_WIDTH
FOX_IN = SWA_IN + N_HEADS
RMS_EPS = 1e-6
NEG_INF = -1e30
ATT_SCALE = HEAD_DIM ** -0.5

kernel_name = 'swa_sink_fox_hybrid_step'


def rmsnorm(x, g):
    xf = x.astype(jnp.float32)
    y = xf * lax.rsqrt(jnp.mean(xf * xf, axis=-1, keepdims=True) + RMS_EPS)
    return (y * g.astype(jnp.float32)).astype(x.dtype)


def alibi_slopes():
    return 2.0 ** (-8.0 * jnp.arange(1, N_HEADS + 1, dtype=jnp.float32) / N_HEADS)


def split_proj(p):
    lead = p.shape[:-1]
    q = p[..., 0:BRANCH].reshape(*lead, N_KV_HEADS, GROUP, HEAD_DIM)
    k = p[..., BRANCH:BRANCH + KV_WIDTH].reshape(*lead, N_KV_HEADS, HEAD_DIM)
    v = p[..., BRANCH + KV_WIDTH:BRANCH + 2 * KV_WIDTH].reshape(*lead, N_KV_HEADS, HEAD_DIM)
    gate = p[..., BRANCH + 2 * KV_WIDTH:SWA_IN]
    extra = p[..., SWA_IN:]
    return q, k, v, gate, extra


def gated_out(o, gate, w_out):
    lead = gate.shape[:-1]
    o = o.reshape(*lead, BRANCH).astype(gate.dtype)
    return (o * jax.nn.silu(gate)) @ w_out


def swa_attend(q, k, v, q_pos, k_pos, slopes, sink):
    s = jnp.einsum('bqhgd,bkhd->bhgqk', q.astype(jnp.float32), k.astype(jnp.float32)) * ATT_SCALE
    dist = q_pos[:, :, None] - k_pos[:, None, :]
    valid = (dist >= 0) & (dist <= WINDOW) & (k_pos[:, None, :] >= 0)
    s = s - slopes.reshape(N_KV_HEADS, GROUP)[None, :, :, None, None] * dist[:, None, None].astype(jnp.float32)
    s = jnp.where(valid[:, None, None], s, NEG_INF)
    sk = sink.astype(jnp.float32).reshape(N_KV_HEADS, GROUP)[None, :, :, None, None]
    m = jnp.maximum(jnp.max(s, axis=-1, keepdims=True), sk)
    p = jnp.exp(s - m)
    den = jnp.sum(p, axis=-1, keepdims=True) + jnp.exp(sk - m)
    return jnp.einsum('bhgqk,bkhd->bqhgd', p / den, v.astype(jnp.float32))


def fox_attend(q, k, v, c_q, c_k, q_pos, k_pos):
    s = jnp.einsum('bqhgd,bkhd->bhgqk', q.astype(jnp.float32), k.astype(jnp.float32)) * ATT_SCALE
    bias = jnp.transpose(c_q, (0, 2, 3, 1))[..., :, None] - jnp.transpose(c_k, (0, 2, 3, 1))[..., None, :]
    causal = q_pos[:, :, None] >= k_pos[:, None, :]
    s = jnp.where(causal[:, None, None], s + bias, NEG_INF)
    p = jax.nn.softmax(s, axis=-1)
    return jnp.einsum('bhgqk,bkhd->bqhgd', p, v.astype(jnp.float32))


def swa_prompt(h, w_in, w_out, sink, slopes):
    B, S, _ = h.shape
    q, k, v, gate, _ = split_proj(h @ w_in)
    nb = S // Q_BLOCK
    qb = q.reshape(B * nb, Q_BLOCK, N_KV_HEADS, GROUP, HEAD_DIM)

    def band(t):
        tb = t.reshape(B, nb, Q_BLOCK, N_KV_HEADS, HEAD_DIM)
        prev = jnp.pad(tb, ((0, 0), (1, 0), (0, 0), (0, 0), (0, 0)))[:, :-1]
        return jnp.concatenate([prev, tb], axis=2).reshape(B * nb, 2 * Q_BLOCK, N_KV_HEADS, HEAD_DIM)

    start = jnp.arange(nb) * Q_BLOCK
    q_pos = start[:, None] + jnp.arange(Q_BLOCK)[None, :]
    k_pos = start[:, None] - Q_BLOCK + jnp.arange(2 * Q_BLOCK)[None, :]
    q_pos = jnp.broadcast_to(q_pos[None], (B, nb, Q_BLOCK)).reshape(B * nb, Q_BLOCK)
    k_pos = jnp.broadcast_to(k_pos[None], (B, nb, 2 * Q_BLOCK)).reshape(B * nb, 2 * Q_BLOCK)
    o = swa_attend(qb, band(k), band(v), q_pos, k_pos, slopes, sink).reshape(B, S, BRANCH)
    return gated_out(o, gate, w_out), k[:, S - WINDOW:], v[:, S - WINDOW:]


def swa_sample(h, buf_k, buf_v, w_in, w_out, sink, slopes):
    Bd, Sd, _ = h.shape
    q, k, v, gate, _ = split_proj(h @ w_in)
    k_all = jnp.concatenate([buf_k.astype(k.dtype), k], axis=1)
    v_all = jnp.concatenate([buf_v.astype(v.dtype), v], axis=1)
    q_pos = (PAST_LEN + jnp.arange(Sd))[None]
    k_pos = (PAST_LEN - WINDOW + jnp.arange(WINDOW + Sd))[None]
    o = swa_attend(q, k_all, v_all, q_pos, k_pos, slopes, sink)
    return gated_out(o, gate, w_out), k_all[:, -WINDOW:], v_all[:, -WINDOW:]


def fox_prompt(h, w_in, b_f, w_out):
    B, S, _ = h.shape
    q, k, v, gate, f_logit = split_proj(h @ w_in)
    logf = jax.nn.log_sigmoid(f_logit.astype(jnp.float32) + b_f.astype(jnp.float32))
    c = jnp.cumsum(logf, axis=1).reshape(B, S, N_KV_HEADS, GROUP)
    nb = S // Q_BLOCK
    qb = jnp.swapaxes(q.reshape(B, nb, Q_BLOCK, N_KV_HEADS, GROUP, HEAD_DIM), 0, 1)
    cb = jnp.swapaxes(c.reshape(B, nb, Q_BLOCK, N_KV_HEADS, GROUP), 0, 1)
    pos = jnp.arange(S)
    pb = pos.reshape(nb, Q_BLOCK)

    def block(args):
        q_i, c_i, p_i = args
        return fox_attend(q_i, k, v, c_i, c, p_i[None], pos[None])

    o = lax.map(block, (qb, cb, pb))
    o = jnp.swapaxes(o, 0, 1).reshape(B, S, BRANCH)
    return gated_out(o, gate, w_out), k, v, logf


def fox_sample(h, cache_k, cache_v, cache_logf, layer, page_table, w_in, b_f, w_out):
    Bd, Sd, _ = h.shape
    q, k, v, gate, f_logit = split_proj(h @ w_in)
    logf = jax.nn.log_sigmoid(f_logit.astype(jnp.float32) + b_f.astype(jnp.float32))
    past = page_table.shape[1] * PAGE_SIZE
    past_k = cache_k[layer, page_table].reshape(Bd, past, N_KV_HEADS, HEAD_DIM)
    past_v = cache_v[layer, page_table].reshape(Bd, past, N_KV_HEADS, HEAD_DIM)
    past_f = cache_logf[layer, page_table].reshape(Bd, past, N_HEADS)
    k_all = jnp.concatenate([past_k.astype(k.dtype), k], axis=1)
    v_all = jnp.concatenate([past_v.astype(v.dtype), v], axis=1)
    f_all = jnp.concatenate([past_f.astype(jnp.float32), logf], axis=1)
    c = jnp.cumsum(f_all, axis=1).reshape(Bd, past + Sd, N_KV_HEADS, GROUP)
    q_pos = (past + jnp.arange(Sd))[None]
    k_pos = jnp.arange(past + Sd)[None]
    o = fox_attend(q, k_all, v_all, c[:, past:], c, q_pos, k_pos)
    return gated_out(o, gate, w_out), k, v, logf


def setup_inputs(seed: int = 0) -> dict:
    key = jax.random.key(seed)
    ks = jax.random.split(key, 16)
    n_pages = PAST_LEN // PAGE_SIZE
    n_pool = (5 * DEC_BATCH * n_pages + 3) // 4
    f32 = jnp.float32
    x_prompt = jax.random.normal(ks[0], (BATCH, SEQ, D_MODEL), f32)
    x_sample = jax.random.normal(ks[1], (DEC_BATCH, DEC_SEQ, D_MODEL), f32)
    state_swa_k = jax.random.normal(ks[2], (N_SWA, DEC_BATCH, WINDOW, N_KV_HEADS, HEAD_DIM), f32)
    state_swa_v = jax.random.normal(ks[3], (N_SWA, DEC_BATCH, WINDOW, N_KV_HEADS, HEAD_DIM), f32)
    cache_fox_k = jax.random.normal(ks[4], (N_FOX, n_pool, PAGE_SIZE, N_KV_HEADS, HEAD_DIM), f32)
    cache_fox_v = jax.random.normal(ks[5], (N_FOX, n_pool, PAGE_SIZE, N_KV_HEADS, HEAD_DIM), f32)
    cache_fox_logf = jax.nn.log_sigmoid(jax.random.normal(ks[6], (N_FOX, n_pool, PAGE_SIZE, N_HEADS), f32) + 3.0)
    page_table = jax.random.permutation(ks[7], n_pool)[:DEC_BATCH * n_pages].reshape(DEC_BATCH, n_pages).astype(jnp.int32)
    norm_pre = 1.0 + 0.02 * jax.random.normal(ks[8], (DEPTH, D_MODEL), f32)
    norm_post = 1.0 + 0.02 * jax.random.normal(ks[9], (DEPTH, D_MODEL), f32)
    w_in_swa = jax.random.normal(ks[10], (N_SWA, D_MODEL, SWA_IN), f32) * D_MODEL ** -0.5
    sinks_swa = jax.random.normal(ks[11], (N_SWA, N_HEADS), f32)
    w_out_swa = jax.random.normal(ks[12], (N_SWA, BRANCH, D_MODEL), f32) * BRANCH ** -0.5
    w_in_fox = jax.random.normal(ks[13], (N_FOX, D_MODEL, FOX_IN), f32) * D_MODEL ** -0.5
    b_forget = jax.random.uniform(ks[14], (N_FOX, N_HEADS), f32, minval=1.0, maxval=4.0)
    w_out_fox = jax.random.normal(ks[15], (N_FOX, BRANCH, D_MODEL), f32) * BRANCH ** -0.5
    return {'x_prompt': x_prompt, 'x_sample': x_sample,
            'state_swa_k': state_swa_k, 'state_swa_v': state_swa_v,
            'cache_fox_k': cache_fox_k, 'cache_fox_v': cache_fox_v, 'cache_fox_logf': cache_fox_logf,
            'page_table': page_table, 'norm_pre': norm_pre, 'norm_post': norm_post,
            'w_in_swa': w_in_swa, 'sinks_swa': sinks_swa, 'w_out_swa': w_out_swa,
            'w_in_fox': w_in_fox, 'b_forget': b_forget, 'w_out_fox': w_out_fox}


def reference(x_prompt, x_sample, state_swa_k, state_swa_v, cache_fox_k, cache_fox_v, cache_fox_logf,
              page_table, norm_pre, norm_post, w_in_swa, sinks_swa, w_out_swa, w_in_fox, b_forget, w_out_fox):
    slopes = alibi_slopes()
    xp, xs = x_prompt, x_sample
    swa_kp, swa_vp, swa_ks, swa_vs = [], [], [], []
    fox_kp, fox_vp, fox_fp, fox_ks, fox_vs, fox_fs = [], [], [], [], [], []
    for i in range(DEPTH):
        j = i // N_MIXERS
        hp = rmsnorm(xp, norm_pre[i])
        hs = rmsnorm(xs, norm_pre[i])
        if i % N_MIXERS == 0:
            yp, kp, vp = swa_prompt(hp, w_in_swa[j], w_out_swa[j], sinks_swa[j], slopes)
            ys, kb, vb = swa_sample(hs, state_swa_k[j], state_swa_v[j], w_in_swa[j], w_out_swa[j], sinks_swa[j], slopes)
            swa_kp.append(kp); swa_vp.append(vp); swa_ks.append(kb); swa_vs.append(vb)
        else:
            yp, kp, vp, fp = fox_prompt(hp, w_in_fox[j], b_forget[j], w_out_fox[j])
            ys, kn, vn, fn = fox_sample(hs, cache_fox_k, cache_fox_v, cache_fox_logf, j, page_table,
                                        w_in_fox[j], b_forget[j], w_out_fox[j])
            fox_kp.append(kp); fox_vp.append(vp); fox_fp.append(fp)
            fox_ks.append(kn); fox_vs.append(vn); fox_fs.append(fn)
        xp = xp + rmsnorm(yp, norm_post[i])
        xs = xs + rmsnorm(ys, norm_post[i])
    return (xp, xs,
            jnp.stack(swa_kp), jnp.stack(swa_vp), jnp.stack(swa_ks), jnp.stack(swa_vs),
            jnp.stack(fox_kp), jnp.stack(fox_vp), jnp.stack(fox_fp),
            jnp.stack(fox_ks), jnp.stack(fox_vs), jnp.stack(fox_fs))
```

```python
import functools

import numpy as np
import jax
import jax.numpy as jnp
from jax import lax
from jax.experimental import pallas as pl
from jax.experimental.pallas import tpu as pltpu

D_MODEL = 1024
N_HEADS = 16
N_KV_HEADS = 4
GROUP = N_HEADS // N_KV_HEADS
HEAD_DIM = 64
BRANCH = N_HEADS * HEAD_DIM
KV_WIDTH = N_KV_HEADS * HEAD_DIM
WINDOW = 128
PAGE = 128
RMS_EPS = 1e-6
NEG_INF = -1e30
ATT_SCALE = HEAD_DIM ** -0.5
LANES = 128
PAD_Q = N_HEADS * LANES
PAD_K = N_KV_HEADS * LANES
N_PIECES = 3
AUG_K0 = HEAD_DIM
AUG_Q0 = HEAD_DIM + N_PIECES * GROUP

PROJ_TM = 256
OUT_TM = 512
FLASH_T = 512
PAGES_PER_STEP = 8
VMEM_LIMIT = 48 * 1024 * 1024

_NT = (((1,), (1,)), ((), ()))


def _dot(a, b):
    return jnp.dot(a, b, preferred_element_type=jnp.float32)


def _dot_nt(a, b):
    return lax.dot_general(a, b, _NT, preferred_element_type=jnp.float32)


def _pieces(x):
    hi = x.astype(jnp.bfloat16)
    r1 = x - hi.astype(jnp.float32)
    mid = r1.astype(jnp.bfloat16)
    lo = (r1 - mid.astype(jnp.float32)).astype(jnp.bfloat16)
    return hi, mid, lo


def _log_sigmoid(z):
    return -(jnp.maximum(-z, 0.0) + jnp.log1p(jnp.exp(-jnp.abs(z))))


def _silu(x):
    return x * (1.0 / (1.0 + jnp.exp(-x)))


def _rmsnorm_rows(x, g):
    ms = jnp.mean(x * x, axis=-1, keepdims=True)
    return x * lax.rsqrt(ms + RMS_EPS) * g


def _iota(shape, dim):
    return lax.broadcasted_iota(jnp.int32, shape, dim)


def _proj_prompt_kernel(*refs, fox, tm):
    if fox:
        (x_ref, g_ref, wqg_ref, wkvt_ref, wf_ref, wft_ref, bf_ref, bft_ref, pselq_ref, constq_ref,
         pselkt_ref, constkt_ref,
         qpad_ref, gate_ref, kt_ref, vt_ref, kta_ref, vt16_ref, logft_ref, carry_ref, carryt_ref) = refs
    else:
        (x_ref, g_ref, wqg_ref, wkvt_ref,
         qpad_ref, gate_ref, kt_ref, vt_ref, kta_ref, vt16_ref) = refs

    hb = _rmsnorm_rows(x_ref[0], g_ref[...]).astype(jnp.bfloat16)
    pq = _dot(hb, wqg_ref[...])
    kvt = _dot_nt(wkvt_ref[...], hb)
    gate_ref[0] = pq[:, BRANCH:]
    kt = kvt[:KV_WIDTH]
    vt = kvt[KV_WIDTH:]
    kt_ref[0] = kt
    vt_ref[0] = vt
    vt16_ref[0] = vt.astype(jnp.bfloat16)

    if fox:
        @pl.when(pl.program_id(1) == 0)
        def _():
            carry_ref[...] = jnp.zeros_like(carry_ref)
            carryt_ref[...] = jnp.zeros_like(carryt_ref)

        logf = _log_sigmoid(_dot(hb, wf_ref[...]) + bf_ref[...])
        logft = _log_sigmoid(_dot_nt(wft_ref[...], hb) + bft_ref[...])
        logft_ref[0] = logft
        row = _iota((tm, tm), 0)
        col = _iota((tm, tm), 1)
        tri = (row >= col).astype(jnp.bfloat16)
        trit = (row <= col).astype(jnp.bfloat16)
        c = carry_ref[...]
        for piece in _pieces(logf):
            c = c + _dot(tri, piece)
        carry_ref[...] = c[tm - 1:tm, :]
        ct = carryt_ref[...]
        for piece in _pieces(logft):
            ct = ct + _dot(piece, trit)
        carryt_ref[...] = carryt_ref[...] + jnp.sum(logft, axis=1, keepdims=True)
        cp = jnp.concatenate(_pieces(c), axis=1)
        aug_q = _dot(cp, pselq_ref[...]) + constq_ref[...]
        zpad = jnp.zeros((LANES - N_PIECES * N_HEADS, tm), jnp.bfloat16)
        cpt = jnp.concatenate(list(_pieces(ct)) + [zpad], axis=0)
        aug_kt = _dot(pselkt_ref[...], cpt) + constkt_ref[...]

    lane = _iota((tm, LANES), 1)
    low = lane < HEAD_DIM
    for pair in range(N_HEADS // 2):
        blk = pq[:, pair * LANES:(pair + 1) * LANES] * ATT_SCALE
        rolled = pltpu.roll(blk, HEAD_DIM, 1)
        for h, val in ((2 * pair, blk), (2 * pair + 1, rolled)):
            spare = aug_q[:, h * LANES:(h + 1) * LANES] if fox else 0.0
            qpad_ref[0, :, h * LANES:(h + 1) * LANES] = jnp.where(low, val, spare).astype(jnp.bfloat16)
    for kvh in range(N_KV_HEADS):
        kth = kt[kvh * HEAD_DIM:(kvh + 1) * HEAD_DIM]
        if fox:
            spare = aug_kt[kvh * LANES + HEAD_DIM:(kvh + 1) * LANES]
        else:
            spare = jnp.zeros((LANES - HEAD_DIM, tm), jnp.float32)
        kta_ref[0, kvh * LANES:(kvh + 1) * LANES, :] = jnp.concatenate([kth, spare], axis=0).astype(jnp.bfloat16)


def _aug_constants():
    pselq = np.zeros((N_PIECES * LANES, PAD_Q), np.float32)
    constq = np.zeros((1, PAD_Q), np.float32)
    pselkt = np.zeros((PAD_K, LANES), np.float32)
    constkt = np.zeros((PAD_K, 1), np.float32)
    for h in range(N_HEADS):
        kvh, g = divmod(h, GROUP)
        for j in range(N_PIECES):
            pselq[j * LANES + h, h * LANES + AUG_Q0 + j] = 1.0
            constq[0, h * LANES + AUG_K0 + N_PIECES * g + j] = 1.0
            pselkt[kvh * LANES + AUG_K0 + N_PIECES * g + j, j * N_HEADS + h] = -1.0
    for kvh in range(N_KV_HEADS):
        for j in range(N_PIECES):
            constkt[kvh * LANES + AUG_Q0 + j, 0] = 1.0
    return (jnp.asarray(pselq, jnp.bfloat16), jnp.asarray(constq), jnp.asarray(pselkt, jnp.bfloat16),
            jnp.asarray(constkt))


def _full(shape):
    return pl.BlockSpec(shape, lambda *_: (0,) * len(shape))


def _proj_prompt(x, g, wqg, wkvt, fox_w=None):
    B, S, _ = x.shape
    tm = PROJ_TM
    fox = fox_w is not None
    ins = [x, g, wqg, wkvt]
    in_specs = [pl.BlockSpec((1, tm, D_MODEL), lambda b, i: (b, i, 0)), _full(g.shape), _full(wqg.shape),
                _full(wkvt.shape)]
    out_shape = [jax.ShapeDtypeStruct((B, S, PAD_Q), jnp.bfloat16),
                 jax.ShapeDtypeStruct((B, S, BRANCH), jnp.float32),
                 jax.ShapeDtypeStruct((B, KV_WIDTH, S), jnp.float32),
                 jax.ShapeDtypeStruct((B, KV_WIDTH, S), jnp.float32),
                 jax.ShapeDtypeStruct((B, PAD_K, S), jnp.bfloat16),
                 jax.ShapeDtypeStruct((B, KV_WIDTH, S), jnp.bfloat16)]
    out_specs = [pl.BlockSpec((1, tm, PAD_Q), lambda b, i: (b, i, 0)),
                 pl.BlockSpec((1, tm, BRANCH), lambda b, i: (b, i, 0)),
                 pl.BlockSpec((1, KV_WIDTH, tm), lambda b, i: (b, 0, i)),
                 pl.BlockSpec((1, KV_WIDTH, tm), lambda b, i: (b, 0, i)),
                 pl.BlockSpec((1, PAD_K, tm), lambda b, i: (b, 0, i)),
                 pl.BlockSpec((1, KV_WIDTH, tm), lambda b, i: (b, 0, i))]
    scratch = []
    if fox:
        extra = list(fox_w) + list(_aug_constants())
        ins += extra
        in_specs += [_full(a.shape) for a in extra]
        out_shape.append(jax.ShapeDtypeStruct((B, N_HEADS, S), jnp.float32))
        out_specs.append(pl.BlockSpec((1, N_HEADS, tm), lambda b, i: (b, 0, i)))
        scratch = [pltpu.VMEM((1, LANES), jnp.float32), pltpu.VMEM((N_HEADS, 1), jnp.float32)]
    return pl.pallas_call(
        functools.partial(_proj_prompt_kernel, fox=fox, tm=tm),
        grid=(B, S // tm),
        in_specs=in_specs,
        out_specs=out_specs,
        out_shape=out_shape,
        scratch_shapes=scratch,
        compiler_params=pltpu.CompilerParams(dimension_semantics=("arbitrary", "arbitrary"),
                                             vmem_limit_bytes=VMEM_LIMIT),
        name="proj_prompt_fox" if fox else "proj_prompt_swa",
    )(*ins)


def _alibi_slopes():
    return [float(2.0 ** (-8.0 * (h + 1) / N_HEADS)) for h in range(N_HEADS)]


def _swa_prompt_kernel(sink_ref, q_ref, kp_ref, kc_ref, vp_ref, vc_ref, gate_ref, o_ref):
    i = pl.program_id(1)
    r = _iota((WINDOW, 2 * WINDOW), 0)
    c = _iota((WINDOW, 2 * WINDOW), 1)
    dist = r - c + WINDOW
    valid = (dist >= 0) & (dist <= WINDOW) & ((c >= WINDOW) | (i > 0))
    distf = dist.astype(jnp.float32)
    slopes = _alibi_slopes()
    zrows = jnp.zeros((LANES - HEAD_DIM, 2 * WINDOW), jnp.bfloat16)
    for pair in range(N_HEADS // 2):
        outs = []
        for h in (2 * pair, 2 * pair + 1):
            kvh = h // GROUP
            q = q_ref[0, :, h * LANES:(h + 1) * LANES]
            kk = jnp.concatenate([kp_ref[0, kvh * LANES:(kvh + 1) * LANES, :],
                                  kc_ref[0, kvh * LANES:(kvh + 1) * LANES, :]], axis=1)
            s = _dot(q, kk) - slopes[h] * distf
            s = jnp.where(valid, s, NEG_INF)
            sk = sink_ref[h]
            m = jnp.maximum(jnp.max(s, axis=1, keepdims=True), sk)
            p = jnp.exp(s - m)
            den = jnp.sum(p, axis=1, keepdims=True) + jnp.exp(sk - m)
            vv = jnp.concatenate([vp_ref[0, kvh * HEAD_DIM:(kvh + 1) * HEAD_DIM, :],
                                  vc_ref[0, kvh * HEAD_DIM:(kvh + 1) * HEAD_DIM, :]], axis=1)
            vv = jnp.concatenate([vv, zrows], axis=0)
            outs.append(_dot_nt(p.astype(jnp.bfloat16), vv) / den)
        o = outs[0] + pltpu.roll(outs[1], HEAD_DIM, 1)
        gt = gate_ref[0, :, pair * LANES:(pair + 1) * LANES]
        o_ref[0, :, pair * LANES:(pair + 1) * LANES] = (o * _silu(gt)).astype(jnp.bfloat16)


def _swa_prompt(sink, qpad, kta, vt16, gate):
    B, S, _ = qpad.shape
    nb = S // WINDOW
    prev = lambda b, i: (b, 0, jnp.maximum(i - 1, 0))
    cur = lambda b, i: (b, 0, i)
    return pl.pallas_call(
        _swa_prompt_kernel,
        grid=(B, nb),
        in_specs=[pl.BlockSpec(memory_space=pltpu.SMEM),
                  pl.BlockSpec((1, WINDOW, PAD_Q), lambda b, i: (b, i, 0)),
                  pl.BlockSpec((1, PAD_K, WINDOW), prev), pl.BlockSpec((1, PAD_K, WINDOW), cur),
                  pl.BlockSpec((1, KV_WIDTH, WINDOW), prev), pl.BlockSpec((1, KV_WIDTH, WINDOW), cur),
                  pl.BlockSpec((1, WINDOW, BRANCH), lambda b, i: (b, i, 0))],
        out_specs=pl.BlockSpec((1, WINDOW, BRANCH), lambda b, i: (b, i, 0)),
        out_shape=jax.ShapeDtypeStruct((B, S, BRANCH), jnp.bfloat16),
        compiler_params=pltpu.CompilerParams(dimension_semantics=("arbitrary", "arbitrary"),
                                             vmem_limit_bytes=VMEM_LIMIT),
        name="swa_prompt",
    )(sink, qpad, kta, kta, vt16, vt16, gate)


def _fox_flash_kernel(qi_ref, kj_ref, q_ref, k_ref, v_ref, gate_ref, o_ref, m_ref, l_ref, acc_ref, *, t):
    step = pl.program_id(2)
    qi = qi_ref[step]
    kj = kj_ref[step]

    @pl.when(kj == 0)
    def _():
        m_ref[...] = jnp.full_like(m_ref, NEG_INF)
        l_ref[...] = jnp.zeros_like(l_ref)
        acc_ref[...] = jnp.zeros_like(acc_ref)

    def update(masked):
        k = k_ref[0]
        v = v_ref[0]
        vpad = jnp.concatenate([v, jnp.zeros_like(v)], axis=0)
        if masked:
            keep = _iota((t, t), 0) >= _iota((t, t), 1)
        for g in range(GROUP):
            s = _dot(q_ref[0, :, g * LANES:(g + 1) * LANES], k)
            if masked:
                s = jnp.where(keep, s, NEG_INF)
            m_prev = m_ref[g]
            m_new = jnp.maximum(m_prev, jnp.max(s, axis=1, keepdims=True))
            alpha = jnp.exp(m_prev - m_new)
            p = jnp.exp(s - m_new)
            l_ref[g] = alpha * l_ref[g] + jnp.sum(p, axis=1, keepdims=True)
            acc_ref[g] = alpha * acc_ref[g] + _dot_nt(p.astype(jnp.bfloat16), vpad)
            m_ref[g] = m_new

    @pl.when(kj < qi)
    def _():
        update(False)

    @pl.when(kj == qi)
    def _():
        update(True)
        for pair in range(GROUP // 2):
            o0 = acc_ref[2 * pair] / l_ref[2 * pair]
            o1 = acc_ref[2 * pair + 1] / l_ref[2 * pair + 1]
            o = o0 + pltpu.roll(o1, HEAD_DIM, 1)
            gt = gate_ref[0, :, pair * LANES:(pair + 1) * LANES]
            o_ref[0, :, pair * LANES:(pair + 1) * LANES] = (o * _silu(gt)).astype(jnp.bfloat16)


def _fox_flash(qpad, kta, vt16, gate):
    B, S, _ = qpad.shape
    t = FLASH_T
    n = S // t
    pairs = [(qi, kj) for qi in range(n) for kj in range(qi + 1)]
    qi_arr = jnp.asarray(np.array([p[0] for p in pairs], np.int32))
    kj_arr = jnp.asarray(np.array([p[1] for p in pairs], np.int32))
    gw = GROUP * HEAD_DIM
    grid_spec = pltpu.PrefetchScalarGridSpec(
        num_scalar_prefetch=2,
        grid=(B, N_KV_HEADS, len(pairs)),
        in_specs=[pl.BlockSpec((1, t, GROUP * LANES), lambda b, h, s, qi, kj: (b, qi[s], h)),
                  pl.BlockSpec((1, LANES, t), lambda b, h, s, qi, kj: (b, h, kj[s])),
                  pl.BlockSpec((1, HEAD_DIM, t), lambda b, h, s, qi, kj: (b, h, kj[s])),
                  pl.BlockSpec((1, t, gw), lambda b, h, s, qi, kj: (b, qi[s], h))],
        out_specs=pl.BlockSpec((1, t, gw), lambda b, h, s, qi, kj: (b, qi[s], h)),
        scratch_shapes=[pltpu.VMEM((GROUP, t, 1), jnp.float32), pltpu.VMEM((GROUP, t, 1), jnp.float32),
                        pltpu.VMEM((GROUP, t, LANES), jnp.float32)])
    return pl.pallas_call(
        functools.partial(_fox_flash_kernel, t=t),
        grid_spec=grid_spec,
        out_shape=jax.ShapeDtypeStruct((B, S, BRANCH), jnp.bfloat16),
        compiler_params=pltpu.CompilerParams(dimension_semantics=("arbitrary", "arbitrary", "arbitrary"),
                                             vmem_limit_bytes=VMEM_LIMIT),
        name="fox_flash",
    )(qi_arr, kj_arr, qpad, kta, vt16, gate)


def _out_proj_kernel(og_ref, w_ref, g_ref, x_ref, o_ref):
    y = _dot(og_ref[...], w_ref[...])
    o_ref[...] = x_ref[...] + _rmsnorm_rows(y, g_ref[...])


def _out_proj(og, w, g, x, tm):
    n = og.shape[0]
    return pl.pallas_call(
        _out_proj_kernel,
        grid=(n // tm,),
        in_specs=[pl.BlockSpec((tm, BRANCH), lambda i: (i, 0)), _full(w.shape), _full(g.shape),
                  pl.BlockSpec((tm, D_MODEL), lambda i: (i, 0))],
        out_specs=pl.BlockSpec((tm, D_MODEL), lambda i: (i, 0)),
        out_shape=jax.ShapeDtypeStruct((n, D_MODEL), jnp.float32),
        compiler_params=pltpu.CompilerParams(dimension_semantics=("arbitrary",), vmem_limit_bytes=VMEM_LIMIT),
        name="out_proj",
    )(og, w, g, x)


def _proj_sample_kernel(*refs, fox):
    if fox:
        x_ref, g_ref, wqg_ref, wkvt_ref, wft_ref, bft_ref, q_ref, gate_ref, kt_ref, vt_ref, logft_ref = refs
    else:
        x_ref, g_ref, wqg_ref, wkvt_ref, q_ref, gate_ref, kt_ref, vt_ref = refs
    hb = _rmsnorm_rows(x_ref[...], g_ref[...]).astype(jnp.bfloat16)
    pq = _dot(hb, wqg_ref[...])
    q_ref[...] = pq[:, :BRANCH] * ATT_SCALE
    gate_ref[...] = pq[:, BRANCH:]
    kvt = _dot_nt(wkvt_ref[...], hb)
    kt_ref[...] = kvt[:KV_WIDTH]
    vt_ref[...] = kvt[KV_WIDTH:]
    if fox:
        logft_ref[...] = _log_sigmoid(_dot_nt(wft_ref[...], hb) + bft_ref[...])


def _proj_sample(x, g, wqg, wkvt, fox_w=None):
    n = x.shape[0]
    fox = fox_w is not None
    ins = [x, g, wqg, wkvt] + (list(fox_w) if fox else [])
    out_shape = [jax.ShapeDtypeStruct((n, BRANCH), jnp.float32), jax.ShapeDtypeStruct((n, BRANCH), jnp.float32),
                 jax.ShapeDtypeStruct((KV_WIDTH, n), jnp.float32), jax.ShapeDtypeStruct((KV_WIDTH, n), jnp.float32)]
    if fox:
        out_shape.append(jax.ShapeDtypeStruct((N_HEADS, n), jnp.float32))
    return pl.pallas_call(
        functools.partial(_proj_sample_kernel, fox=fox),
        grid=(1,),
        in_specs=[_full(a.shape) for a in ins],
        out_specs=[_full(s.shape) for s in out_shape],
        out_shape=out_shape,
        compiler_params=pltpu.CompilerParams(dimension_semantics=("arbitrary",), vmem_limit_bytes=VMEM_LIMIT),
        name="proj_sample_fox" if fox else "proj_sample_swa",
    )(*ins)


def _head_placement():
    r = np.zeros((BRANCH, KV_WIDTH), np.float32)
    for h in range(N_HEADS):
        for d in range(HEAD_DIM):
            r[h * HEAD_DIM + d, (h // GROUP) * HEAD_DIM + d] = 1.0
    return jnp.asarray(r, jnp.bfloat16), jnp.asarray(r.T, jnp.bfloat16)


def _own_lanes_mask():
    return (_iota((N_HEADS, BRANCH), 1) // HEAD_DIM) == _iota((N_HEADS, BRANCH), 0)


def _block_diag_q(qrow, r):
    dq = jnp.where(_own_lanes_mask(), jnp.broadcast_to(qrow, (N_HEADS, BRANCH)), 0.0).astype(jnp.bfloat16)
    return _dot(dq, r).astype(jnp.bfloat16)


def _gated_row(o16, gate_row, rt):
    own_kv = (_iota((N_HEADS, KV_WIDTH), 1) // HEAD_DIM) == (_iota((N_HEADS, KV_WIDTH), 0) // GROUP)
    om = jnp.where(own_kv, o16, 0.0)
    e = jnp.zeros((N_HEADS, BRANCH), jnp.float32)
    for piece in _pieces(om):
        e = e + _dot(piece, rt)
    orow = jnp.sum(jnp.where(_own_lanes_mask(), e, 0.0), axis=0, keepdims=True)
    return (orow * _silu(gate_row)).astype(jnp.bfloat16)


def _pick_lane(x, b):
    return jnp.sum(jnp.where(_iota(x.shape, 1) == b, x, 0.0), axis=1, keepdims=True)


def _swa_sample_kernel(q_ref, gate_ref, kbuf_ref, vbuf_ref, ktn_ref, vtn_ref, sink_ref, slope_ref, r_ref, rt_ref,
                       o_ref, kout_ref, vout_ref):
    b = pl.program_id(0)
    qbd = _block_diag_q(q_ref[0], r_ref[...])
    kt = kbuf_ref[0]
    vt = vbuf_ref[0]
    ktn = ktn_ref[...]
    vtn = vtn_ref[...]
    lane = _iota((N_HEADS, WINDOW), 1)
    dist = (WINDOW - lane).astype(jnp.float32)
    s_old = _dot(qbd, kt.astype(jnp.bfloat16)) - slope_ref[...] * dist
    s_new = jnp.where(lane == b, _dot(qbd, ktn.astype(jnp.bfloat16)), NEG_INF)
    sk = sink_ref[...]
    m = jnp.maximum(jnp.maximum(jnp.max(s_old, axis=1, keepdims=True), jnp.max(s_new, axis=1, keepdims=True)), sk)
    p_old = jnp.exp(s_old - m)
    p_new = jnp.exp(s_new - m)
    den = jnp.sum(p_old, axis=1, keepdims=True) + jnp.sum(p_new, axis=1, keepdims=True) + jnp.exp(sk - m)
    o16 = (_dot_nt(p_old.astype(jnp.bfloat16), vt.astype(jnp.bfloat16))
           + _dot_nt(p_new.astype(jnp.bfloat16), vtn.astype(jnp.bfloat16))) / den
    o_ref[0] = _gated_row(o16, gate_ref[0], rt_ref[...])
    last = _iota((KV_WIDTH, WINDOW), 1) == WINDOW - 1
    kout_ref[0] = jnp.where(last, _pick_lane(ktn, b), pltpu.roll(kt, WINDOW - 1, 1))
    vout_ref[0] = jnp.where(last, _pick_lane(vtn, b), pltpu.roll(vt, WINDOW - 1, 1))


def _swa_sample(q, gate, kbuf, vbuf, ktn, vtn, sink, slope, r, rt):
    n = q.shape[0]
    row = pl.BlockSpec((1, 1, BRANCH), lambda b: (b, 0, 0))
    buf = pl.BlockSpec((1, KV_WIDTH, WINDOW), lambda b: (b, 0, 0))
    return pl.pallas_call(
        _swa_sample_kernel,
        grid=(n,),
        in_specs=[row, row, buf, buf, _full(ktn.shape), _full(vtn.shape), _full(sink.shape), _full(slope.shape),
                  _full(r.shape), _full(rt.shape)],
        out_specs=[row, buf, buf],
        out_shape=[jax.ShapeDtypeStruct((n, 1, BRANCH), jnp.bfloat16),
                   jax.ShapeDtypeStruct(kbuf.shape, jnp.float32), jax.ShapeDtypeStruct(vbuf.shape, jnp.float32)],
        compiler_params=pltpu.CompilerParams(dimension_semantics=("arbitrary",), vmem_limit_bytes=VMEM_LIMIT),
        name="swa_sample",
    )(q, gate, kbuf, vbuf, ktn, vtn, sink, slope, r, rt)


def _fox_sample_kernel(pt_ref, q_ref, gate_ref, ktn_ref, vtn_ref, lfn_ref, r_ref, rt_ref, *rest, n_steps):
    npg = PAGES_PER_STEP
    k_refs = rest[:npg]
    v_refs = rest[npg:2 * npg]
    lf_refs = rest[2 * npg:3 * npg]
    o_ref, qbd_ref, m_ref, l_ref, acc_ref, carry_ref = rest[3 * npg:]
    b = pl.program_id(0)
    c = pl.program_id(1)

    @pl.when(c == 0)
    def _():
        qbd = _block_diag_q(q_ref[0], r_ref[...])
        qbd_ref[...] = qbd
        lane = _iota((N_HEADS, LANES), 1)
        s = jnp.where(lane == b, _dot(qbd, ktn_ref[...].astype(jnp.bfloat16)), NEG_INF)
        m = jnp.max(s, axis=1, keepdims=True)
        p = jnp.exp(s - m)
        m_ref[...] = m
        l_ref[...] = jnp.sum(p, axis=1, keepdims=True)
        acc_ref[...] = _dot_nt(p.astype(jnp.bfloat16), vtn_ref[...].astype(jnp.bfloat16))
        carry_ref[...] = _pick_lane(lfn_ref[...], b)

    qbd = qbd_ref[...]
    later = (_iota((PAGE, PAGE), 0) > _iota((PAGE, PAGE), 1)).astype(jnp.bfloat16)
    carry = carry_ref[...]
    scores = []
    for i in range(npg):
        lf = lf_refs[i][0, 0]
        bias = carry
        for piece in _pieces(lf):
            bias = bias + _dot(piece, later)
        carry = carry + jnp.sum(lf, axis=1, keepdims=True)
        scores.append(_dot(qbd, k_refs[i][0, 0].astype(jnp.bfloat16)) + bias)
    carry_ref[...] = carry
    m_prev = m_ref[...]
    m_new = m_prev
    for s in scores:
        m_new = jnp.maximum(m_new, jnp.max(s, axis=1, keepdims=True))
    alpha = jnp.exp(m_prev - m_new)
    l = alpha * l_ref[...]
    acc = alpha * acc_ref[...]
    for i in range(npg):
        p = jnp.exp(scores[i] - m_new)
        l = l + jnp.sum(p, axis=1, keepdims=True)
        acc = acc + _dot_nt(p.astype(jnp.bfloat16), v_refs[i][0, 0].astype(jnp.bfloat16))
    m_ref[...] = m_new
    l_ref[...] = l
    acc_ref[...] = acc

    @pl.when(c == n_steps - 1)
    def _():
        o_ref[0] = _gated_row(acc / l, gate_ref[0], rt_ref[...])


def _fox_sample(page_table, q, gate, ktn, vtn, lfn, r, rt, cache_kt, cache_vt, cache_lft, layer):
    n, n_pages = page_table.shape
    npg = PAGES_PER_STEP
    n_steps = n_pages // npg
    row = pl.BlockSpec((1, 1, BRANCH), lambda b, c, pt: (b, 0, 0))

    def page_spec(rows, i):
        return pl.BlockSpec((1, 1, rows, PAGE),
                            lambda b, c, pt: (layer, pt[b, n_pages - 1 - (c * npg + i)], 0, 0))

    const = lambda a: pl.BlockSpec(a.shape, lambda b, c, pt: (0,) * a.ndim)
    in_specs = ([row, row, const(ktn), const(vtn), const(lfn), const(r), const(rt)]
                + [page_spec(KV_WIDTH, i) for i in range(npg)]
                + [page_spec(KV_WIDTH, i) for i in range(npg)]
                + [page_spec(N_HEADS, i) for i in range(npg)])
    grid_spec = pltpu.PrefetchScalarGridSpec(
        num_scalar_prefetch=1,
        grid=(n, n_steps),
        in_specs=in_specs,
        out_specs=row,
        scratch_shapes=[pltpu.VMEM((N_HEADS, KV_WIDTH), jnp.bfloat16), pltpu.VMEM((N_HEADS, 1), jnp.float32),
                        pltpu.VMEM((N_HEADS, 1), jnp.float32), pltpu.VMEM((N_HEADS, KV_WIDTH), jnp.float32),
                        pltpu.VMEM((N_HEADS, 1), jnp.float32)])
    return pl.pallas_call(
        functools.partial(_fox_sample_kernel, n_steps=n_steps),
        grid_spec=grid_spec,
        out_shape=jax.ShapeDtypeStruct((n, 1, BRANCH), jnp.bfloat16),
        compiler_params=pltpu.CompilerParams(dimension_semantics=("arbitrary", "arbitrary"),
                                             vmem_limit_bytes=VMEM_LIMIT),
        name="fox_sample",
    )(page_table, q, gate, ktn, vtn, lfn, r, rt, *([cache_kt] * npg), *([cache_vt] * npg), *([cache_lft] * npg))


def _feature_major(t):
    lead = t.shape[:-3]
    n = len(lead)
    t = jnp.transpose(t, tuple(range(n)) + (n + 1, n + 2, n))
    return t.reshape(*lead, KV_WIDTH, t.shape[-1])


def _row_major(t):
    lead = t.shape[:-2]
    n = len(lead)
    t = t.reshape(*lead, N_KV_HEADS, HEAD_DIM, t.shape[-1])
    return jnp.transpose(t, tuple(range(n)) + (n + 2, n, n + 1))


def kernel(x_prompt, x_sample, state_swa_k, state_swa_v, cache_fox_k, cache_fox_v, cache_fox_logf, page_table,
           norm_pre, norm_post, w_in_swa, sinks_swa, w_out_swa, w_in_fox, b_forget, w_out_fox):
    B, S, _ = x_prompt.shape
    n_dec = x_sample.shape[0]
    depth = norm_pre.shape[0]
    bf16 = jnp.bfloat16
    r, rt = _head_placement()
    slope_col = jnp.asarray(np.array(_alibi_slopes(), np.float32).reshape(N_HEADS, 1))
    cache_kt = _feature_major(cache_fox_k)
    cache_vt = _feature_major(cache_fox_v)
    cache_lft = jnp.swapaxes(cache_fox_logf, -1, -2)

    xp = x_prompt
    xs = x_sample.reshape(n_dec, D_MODEL)
    swa_kp, swa_vp, swa_ks, swa_vs = [], [], [], []
    fox_kp, fox_vp, fox_fp, fox_ks, fox_vs, fox_fs = [], [], [], [], [], []
    for i in range(depth):
        j = i // 2
        fox = i % 2 == 1
        w_in = w_in_fox[j] if fox else w_in_swa[j]
        w_out = (w_out_fox[j] if fox else w_out_swa[j]).astype(bf16)
        wqg = jnp.concatenate([w_in[:, :BRANCH], w_in[:, BRANCH + 2 * KV_WIDTH:2 * BRANCH + 2 * KV_WIDTH]],
                              axis=1).astype(bf16)
        wkvt = w_in[:, BRANCH:BRANCH + 2 * KV_WIDTH].T.astype(bf16)
        g_pre = norm_pre[i].reshape(1, D_MODEL)
        g_post = norm_post[i].reshape(1, D_MODEL)
        if fox:
            wf = w_in[:, 2 * BRANCH + 2 * KV_WIDTH:]
            wf_pad = jnp.pad(wf, ((0, 0), (0, LANES - N_HEADS))).astype(bf16)
            wft = wf.T.astype(bf16)
            bf_row = jnp.pad(b_forget[j], (0, LANES - N_HEADS)).reshape(1, LANES)
            bf_col = b_forget[j].reshape(N_HEADS, 1)
            qpad, gate, kt, vt, kta, vt16, logft = _proj_prompt(xp, g_pre, wqg, wkvt, (wf_pad, wft, bf_row, bf_col))
            og = _fox_flash(qpad, kta, vt16, gate)
            qs, gs, ktn, vtn, lfn = _proj_sample(xs, g_pre, wqg, wkvt, (wft, bf_col))
            ogs = _fox_sample(page_table, qs.reshape(n_dec, 1, BRANCH), gs.reshape(n_dec, 1, BRANCH), ktn, vtn, lfn,
                              r, rt, cache_kt, cache_vt, cache_lft, j)
            fox_kp.append(_row_major(kt)); fox_vp.append(_row_major(vt)); fox_fp.append(jnp.swapaxes(logft, 1, 2))
            fox_ks.append(_row_major(ktn[None])[0][:, None]); fox_vs.append(_row_major(vtn[None])[0][:, None])
            fox_fs.append(lfn.T[:, None, :])
        else:
            qpad, gate, kt, vt, kta, vt16 = _proj_prompt(xp, g_pre, wqg, wkvt)
            og = _swa_prompt(sinks_swa[j], qpad, kta, vt16, gate)
            qs, gs, ktn, vtn = _proj_sample(xs, g_pre, wqg, wkvt)
            ogs, kout, vout = _swa_sample(qs.reshape(n_dec, 1, BRANCH), gs.reshape(n_dec, 1, BRANCH),
                                          _feature_major(state_swa_k[j]), _feature_major(state_swa_v[j]),
                                          ktn, vtn, sinks_swa[j].reshape(N_HEADS, 1), slope_col, r, rt)
            swa_kp.append(_row_major(kt[:, :, S - WINDOW:])); swa_vp.append(_row_major(vt[:, :, S - WINDOW:]))
            swa_ks.append(_row_major(kout)); swa_vs.append(_row_major(vout))
        xp = _out_proj(og.reshape(B * S, BRANCH), w_out, g_post, xp.reshape(B * S, D_MODEL), OUT_TM).reshape(B, S, D_MODEL)
        xs = _out_proj(ogs.reshape(n_dec, BRANCH), w_out, g_post, xs, n_dec)
    return (xp, xs.reshape(n_dec, 1, D_MODEL),
            jnp.stack(swa_kp), jnp.stack(swa_vp), jnp.stack(swa_ks), jnp.stack(swa_vs),
            jnp.stack(fox_kp), jnp.stack(fox_vp), jnp.stack(fox_fp),
            jnp.stack(fox_ks), jnp.stack(fox_vs), jnp.stack(fox_fs))
```

```python
import functools

import numpy as np
import jax
import jax.numpy as jnp
from jax import lax
from jax.experimental import pallas as pl
from jax.experimental.pallas import tpu as pltpu

D_MODEL = 1024
N_HEADS = 16
N_KV_HEADS = 4
GROUP = N_HEADS // N_KV_HEADS
HEAD_DIM = 64
BRANCH = N_HEADS * HEAD_DIM
KV_WIDTH = N_KV_HEADS * HEAD_DIM
WINDOW = 128
PAGE = 128
RMS_EPS = 1e-6
NEG_INF = -1e30
ATT_SCALE = HEAD_DIM ** -0.5
LANES = 128
PAD_Q = N_HEADS * LANES
PAD_K = N_KV_HEADS * LANES
N_PIECES = 3
AUG_K0 = HEAD_DIM
AUG_Q0 = HEAD_DIM + N_PIECES * GROUP

PROJ_TM = 256
OUT_TM = 512
FLASH_T = 512
PAGES_PER_CHUNK = 16
VMEM_LIMIT = 48 * 1024 * 1024

_NT = (((1,), (1,)), ((), ()))
_TN = (((0,), (0,)), ((), ()))


def _dot(a, b):
    return jnp.dot(a, b, preferred_element_type=jnp.float32)


def _dot_nt(a, b):
    return lax.dot_general(a, b, _NT, preferred_element_type=jnp.float32)


def _pieces(x):
    hi = x.astype(jnp.bfloat16)
    r1 = x - hi.astype(jnp.float32)
    mid = r1.astype(jnp.bfloat16)
    lo = (r1 - mid.astype(jnp.float32)).astype(jnp.bfloat16)
    return hi, mid, lo


def _log_sigmoid(z):
    return -(jnp.maximum(-z, 0.0) + jnp.log1p(jnp.exp(-jnp.abs(z))))


def _silu(x):
    return x * (1.0 / (1.0 + jnp.exp(-x)))


def _rmsnorm_rows(x, g):
    ms = jnp.mean(x * x, axis=-1, keepdims=True)
    return x * lax.rsqrt(ms + RMS_EPS) * g


def _iota(shape, dim):
    return lax.broadcasted_iota(jnp.int32, shape, dim)


def _proj_prompt_kernel(*refs, fox, tm):
    if fox:
        (x_ref, g_ref, wallt_ref, wk_ref, wf_ref, wft_ref, bf_ref, bft_ref, pselk_ref, constk_ref,
         pselqt_ref, constqt_ref,
         qt_ref, gatet_ref, kt_ref, vt_ref, ka_ref, vt16_ref, logft_ref, carry_ref, carryt_ref) = refs
    else:
        (x_ref, g_ref, wallt_ref, wk_ref,
         qt_ref, gatet_ref, kt_ref, vt_ref, ka_ref, vt16_ref) = refs

    hb = _rmsnorm_rows(x_ref[0], g_ref[...]).astype(jnp.bfloat16)
    allt = _dot_nt(wallt_ref[...], hb)
    k_rows = _dot(hb, wk_ref[...])
    kt = allt[BRANCH:BRANCH + KV_WIDTH]
    vt = allt[BRANCH + KV_WIDTH:BRANCH + 2 * KV_WIDTH]
    gatet_ref[0] = allt[BRANCH + 2 * KV_WIDTH:]
    kt_ref[0] = kt
    vt_ref[0] = vt
    vt16_ref[0] = vt.astype(jnp.bfloat16)

    if fox:
        @pl.when(pl.program_id(1) == 0)
        def _():
            carry_ref[...] = jnp.zeros_like(carry_ref)
            carryt_ref[...] = jnp.zeros_like(carryt_ref)

        logf = _log_sigmoid(_dot(hb, wf_ref[...]) + bf_ref[...])
        logft = _log_sigmoid(_dot_nt(wft_ref[...], hb) + bft_ref[...])
        logft_ref[0] = logft
        row = _iota((tm, tm), 0)
        col = _iota((tm, tm), 1)
        tri = (row >= col).astype(jnp.bfloat16)
        trit = (row <= col).astype(jnp.bfloat16)
        c = carry_ref[...]
        for piece in _pieces(logf):
            c = c + _dot(tri, piece)
        carry_ref[...] = c[tm - 1:tm, :]
        ct = carryt_ref[...]
        for piece in _pieces(logft):
            ct = ct + _dot(piece, trit)
        carryt_ref[...] = carryt_ref[...] + jnp.sum(logft, axis=1, keepdims=True)
        cp = jnp.concatenate(_pieces(c), axis=1)
        aug_k = _dot(cp, pselk_ref[...]) + constk_ref[...]
        zpad = jnp.zeros((LANES - N_PIECES * N_HEADS, tm), jnp.bfloat16)
        cpt = jnp.concatenate(list(_pieces(ct)) + [zpad], axis=0)
        aug_qt = _dot(pselqt_ref[...], cpt) + constqt_ref[...]

    spare_rows = LANES - HEAD_DIM
    for h in range(N_HEADS):
        qt_ref[0, h * LANES:h * LANES + HEAD_DIM, :] = (
            allt[h * HEAD_DIM:(h + 1) * HEAD_DIM] * ATT_SCALE).astype(jnp.bfloat16)
        if fox:
            spare = aug_qt[h * spare_rows:(h + 1) * spare_rows].astype(jnp.bfloat16)
        else:
            spare = jnp.zeros((spare_rows, tm), jnp.bfloat16)
        qt_ref[0, h * LANES + HEAD_DIM:(h + 1) * LANES, :] = spare
    low = _iota((tm, LANES), 1) < HEAD_DIM
    for pair in range(N_KV_HEADS // 2):
        blk = k_rows[:, pair * LANES:(pair + 1) * LANES]
        rolled = pltpu.roll(blk, HEAD_DIM, 1)
        for kvh, val in ((2 * pair, blk), (2 * pair + 1, rolled)):
            spare = aug_k[:, kvh * LANES:(kvh + 1) * LANES] if fox else 0.0
            ka_ref[0, :, kvh * LANES:(kvh + 1) * LANES] = jnp.where(low, val, spare).astype(jnp.bfloat16)


def _aug_constants():
    spare_rows = LANES - HEAD_DIM
    pselk = np.zeros((N_PIECES * LANES, PAD_K), np.float32)
    constk = np.zeros((1, PAD_K), np.float32)
    pselqt = np.zeros((N_HEADS * spare_rows, LANES), np.float32)
    constqt = np.zeros((N_HEADS * spare_rows, 1), np.float32)
    for h in range(N_HEADS):
        kvh, g = divmod(h, GROUP)
        for j in range(N_PIECES):
            pselk[j * LANES + h, kvh * LANES + AUG_K0 + N_PIECES * g + j] = -1.0
            pselqt[h * spare_rows + AUG_Q0 - HEAD_DIM + j, j * N_HEADS + h] = 1.0
            constqt[h * spare_rows + AUG_K0 - HEAD_DIM + N_PIECES * g + j, 0] = 1.0
    for kvh in range(N_KV_HEADS):
        for j in range(N_PIECES):
            constk[0, kvh * LANES + AUG_Q0 + j] = 1.0
    return (jnp.asarray(pselk, jnp.bfloat16), jnp.asarray(constk), jnp.asarray(pselqt, jnp.bfloat16),
            jnp.asarray(constqt))


def _full(shape):
    return pl.BlockSpec(shape, lambda *_: (0,) * len(shape))


def _proj_prompt(x, g, wallt, wk, fox_w=None):
    B, S, _ = x.shape
    tm = PROJ_TM
    fox = fox_w is not None
    ins = [x, g, wallt, wk]
    in_specs = [pl.BlockSpec((1, tm, D_MODEL), lambda b, i: (b, i, 0)), _full(g.shape), _full(wallt.shape),
                _full(wk.shape)]
    feat = lambda rows: pl.BlockSpec((1, rows, tm), lambda b, i: (b, 0, i))
    out_shape = [jax.ShapeDtypeStruct((B, PAD_Q, S), jnp.bfloat16),
                 jax.ShapeDtypeStruct((B, BRANCH, S), jnp.float32),
                 jax.ShapeDtypeStruct((B, KV_WIDTH, S), jnp.float32),
                 jax.ShapeDtypeStruct((B, KV_WIDTH, S), jnp.float32),
                 jax.ShapeDtypeStruct((B, S, PAD_K), jnp.bfloat16),
                 jax.ShapeDtypeStruct((B, KV_WIDTH, S), jnp.bfloat16)]
    out_specs = [feat(PAD_Q), feat(BRANCH), feat(KV_WIDTH), feat(KV_WIDTH),
                 pl.BlockSpec((1, tm, PAD_K), lambda b, i: (b, i, 0)), feat(KV_WIDTH)]
    scratch = []
    if fox:
        extra = list(fox_w) + list(_aug_constants())
        ins += extra
        in_specs += [_full(a.shape) for a in extra]
        out_shape.append(jax.ShapeDtypeStruct((B, N_HEADS, S), jnp.float32))
        out_specs.append(feat(N_HEADS))
        scratch = [pltpu.VMEM((1, LANES), jnp.float32), pltpu.VMEM((N_HEADS, 1), jnp.float32)]
    return pl.pallas_call(
        functools.partial(_proj_prompt_kernel, fox=fox, tm=tm),
        grid=(B, S // tm),
        in_specs=in_specs,
        out_specs=out_specs,
        out_shape=out_shape,
        scratch_shapes=scratch,
        compiler_params=pltpu.CompilerParams(dimension_semantics=("arbitrary", "arbitrary"),
                                             vmem_limit_bytes=VMEM_LIMIT),
        name="proj_prompt_fox" if fox else "proj_prompt_swa",
    )(*ins)


def _alibi_slopes():
    return [float(2.0 ** (-8.0 * (h + 1) / N_HEADS)) for h in range(N_HEADS)]


def _swa_prompt_kernel(sink_ref, qt_ref, kp_ref, kc_ref, vp_ref, vc_ref, gatet_ref, o_ref):
    i = pl.program_id(1)
    r = _iota((2 * WINDOW, WINDOW), 0)
    c = _iota((2 * WINDOW, WINDOW), 1)
    dist = c - r + WINDOW
    valid = (dist >= 0) & (dist <= WINDOW) & ((r >= WINDOW) | (i > 0))
    distf = dist.astype(jnp.float32)
    slopes = _alibi_slopes()
    for h in range(N_HEADS):
        kvh = h // GROUP
        kk = jnp.concatenate([kp_ref[0, :, kvh * LANES:(kvh + 1) * LANES],
                              kc_ref[0, :, kvh * LANES:(kvh + 1) * LANES]], axis=0)
        st = _dot(kk, qt_ref[0, h * LANES:(h + 1) * LANES, :]) - slopes[h] * distf
        st = jnp.where(valid, st, NEG_INF)
        sk = sink_ref[h]
        m = jnp.maximum(jnp.max(st, axis=0, keepdims=True), sk)
        p = jnp.exp(st - m)
        den = jnp.sum(p, axis=0, keepdims=True) + jnp.exp(sk - m)
        vv = jnp.concatenate([vp_ref[0, kvh * HEAD_DIM:(kvh + 1) * HEAD_DIM, :],
                              vc_ref[0, kvh * HEAD_DIM:(kvh + 1) * HEAD_DIM, :]], axis=1)
        ot = _dot(vv, p.astype(jnp.bfloat16)) / den
        gt = gatet_ref[0, h * HEAD_DIM:(h + 1) * HEAD_DIM, :]
        o_ref[0, h * HEAD_DIM:(h + 1) * HEAD_DIM, :] = (ot * _silu(gt)).astype(jnp.bfloat16)


def _swa_prompt(sink, qt, ka, vt16, gatet):
    B, _, S = qt.shape
    nb = S // WINDOW
    feat = lambda rows: pl.BlockSpec((1, rows, WINDOW), lambda b, i: (b, 0, i))
    return pl.pallas_call(
        _swa_prompt_kernel,
        grid=(B, nb),
        in_specs=[pl.BlockSpec(memory_space=pltpu.SMEM),
                  feat(PAD_Q),
                  pl.BlockSpec((1, WINDOW, PAD_K), lambda b, i: (b, jnp.maximum(i - 1, 0), 0)),
                  pl.BlockSpec((1, WINDOW, PAD_K), lambda b, i: (b, i, 0)),
                  pl.BlockSpec((1, KV_WIDTH, WINDOW), lambda b, i: (b, 0, jnp.maximum(i - 1, 0))),
                  feat(KV_WIDTH),
                  feat(BRANCH)],
        out_specs=feat(BRANCH),
        out_shape=jax.ShapeDtypeStruct((B, BRANCH, S), jnp.bfloat16),
        compiler_params=pltpu.CompilerParams(dimension_semantics=("arbitrary", "arbitrary"),
                                             vmem_limit_bytes=VMEM_LIMIT),
        name="swa_prompt",
    )(sink, qt, ka, ka, vt16, vt16, gatet)


def _fox_flash_kernel(qi_ref, kj_ref, qt_ref, k_ref, v_ref, gatet_ref, o_ref, m_ref, l_ref, acc_ref, *, t):
    step = pl.program_id(2)
    qi = qi_ref[step]
    kj = kj_ref[step]

    @pl.when(kj == 0)
    def _():
        m_ref[...] = jnp.full_like(m_ref, NEG_INF)
        l_ref[...] = jnp.zeros_like(l_ref)
        acc_ref[...] = jnp.zeros_like(acc_ref)

    def update(masked):
        k = k_ref[0]
        v = v_ref[0]
        if masked:
            keep = _iota((t, t), 0) <= _iota((t, t), 1)
        for g in range(GROUP):
            st = _dot(k, qt_ref[0, g * LANES:(g + 1) * LANES, :])
            if masked:
                st = jnp.where(keep, st, NEG_INF)
            m_prev = m_ref[g]
            m_new = jnp.maximum(m_prev, jnp.max(st, axis=0, keepdims=True))
            alpha = jnp.exp(m_prev - m_new)
            p = jnp.exp(st - m_new)
            l_ref[g] = alpha * l_ref[g] + jnp.sum(p, axis=0, keepdims=True)
            acc_ref[g] = alpha * acc_ref[g] + _dot(v, p.astype(jnp.bfloat16))
            m_ref[g] = m_new

    @pl.when(kj < qi)
    def _():
        update(False)

    @pl.when(kj == qi)
    def _():
        update(True)
        for g in range(GROUP):
            o = acc_ref[g] / l_ref[g]
            gt = gatet_ref[0, g * HEAD_DIM:(g + 1) * HEAD_DIM, :]
            o_ref[0, g * HEAD_DIM:(g + 1) * HEAD_DIM, :] = (o * _silu(gt)).astype(jnp.bfloat16)


def _fox_flash(qt, ka, vt16, gatet):
    B, _, S = qt.shape
    t = FLASH_T
    n = S // t
    pairs = [(qi, kj) for qi in range(n) for kj in range(qi + 1)]
    qi_arr = jnp.asarray(np.array([p[0] for p in pairs], np.int32))
    kj_arr = jnp.asarray(np.array([p[1] for p in pairs], np.int32))
    gw = GROUP * HEAD_DIM
    grid_spec = pltpu.PrefetchScalarGridSpec(
        num_scalar_prefetch=2,
        grid=(B, N_KV_HEADS, len(pairs)),
        in_specs=[pl.BlockSpec((1, GROUP * LANES, t), lambda b, h, s, qi, kj: (b, h, qi[s])),
                  pl.BlockSpec((1, t, LANES), lambda b, h, s, qi, kj: (b, kj[s], h)),
                  pl.BlockSpec((1, HEAD_DIM, t), lambda b, h, s, qi, kj: (b, h, kj[s])),
                  pl.BlockSpec((1, gw, t), lambda b, h, s, qi, kj: (b, h, qi[s]))],
        out_specs=pl.BlockSpec((1, gw, t), lambda b, h, s, qi, kj: (b, h, qi[s])),
        scratch_shapes=[pltpu.VMEM((GROUP, 1, t), jnp.float32), pltpu.VMEM((GROUP, 1, t), jnp.float32),
                        pltpu.VMEM((GROUP, HEAD_DIM, t), jnp.float32)])
    return pl.pallas_call(
        functools.partial(_fox_flash_kernel, t=t),
        grid_spec=grid_spec,
        out_shape=jax.ShapeDtypeStruct((B, BRANCH, S), jnp.bfloat16),
        compiler_params=pltpu.CompilerParams(dimension_semantics=("arbitrary", "arbitrary", "arbitrary"),
                                             vmem_limit_bytes=VMEM_LIMIT),
        name="fox_flash",
    )(qi_arr, kj_arr, qt, ka, vt16, gatet)


def _out_proj_kernel(og_ref, w_ref, g_ref, x_ref, o_ref, *, feature_major):
    if feature_major:
        y = lax.dot_general(og_ref[0], w_ref[...], _TN, preferred_element_type=jnp.float32)
        o_ref[0] = x_ref[0] + _rmsnorm_rows(y, g_ref[...])
    else:
        y = _dot(og_ref[...], w_ref[...])
        o_ref[...] = x_ref[...] + _rmsnorm_rows(y, g_ref[...])


def _out_proj_prompt(ogt, w, g, x):
    B, S, _ = x.shape
    tm = OUT_TM
    rows = pl.BlockSpec((1, tm, D_MODEL), lambda b, i: (b, i, 0))
    return pl.pallas_call(
        functools.partial(_out_proj_kernel, feature_major=True),
        grid=(B, S // tm),
        in_specs=[pl.BlockSpec((1, BRANCH, tm), lambda b, i: (b, 0, i)), _full(w.shape), _full(g.shape), rows],
        out_specs=rows,
        out_shape=jax.ShapeDtypeStruct(x.shape, jnp.float32),
        compiler_params=pltpu.CompilerParams(dimension_semantics=("arbitrary", "arbitrary"),
                                             vmem_limit_bytes=VMEM_LIMIT),
        name="out_proj_prompt",
    )(ogt, w, g, x)


def _out_proj_sample(og, w, g, x):
    return pl.pallas_call(
        functools.partial(_out_proj_kernel, feature_major=False),
        grid=(1,),
        in_specs=[_full(og.shape), _full(w.shape), _full(g.shape), _full(x.shape)],
        out_specs=_full(x.shape),
        out_shape=jax.ShapeDtypeStruct(x.shape, jnp.float32),
        compiler_params=pltpu.CompilerParams(dimension_semantics=("arbitrary",), vmem_limit_bytes=VMEM_LIMIT),
        name="out_proj_sample",
    )(og, w, g, x)


def _proj_sample_kernel(*refs, fox):
    if fox:
        x_ref, g_ref, wqg_ref, wkvt_ref, wft_ref, bft_ref, q_ref, gate_ref, kt_ref, vt_ref, logft_ref = refs
    else:
        x_ref, g_ref, wqg_ref, wkvt_ref, q_ref, gate_ref, kt_ref, vt_ref = refs
    hb = _rmsnorm_rows(x_ref[...], g_ref[...]).astype(jnp.bfloat16)
    pq = _dot(hb, wqg_ref[...])
    q_ref[...] = pq[:, :BRANCH] * ATT_SCALE
    gate_ref[...] = pq[:, BRANCH:]
    kvt = _dot_nt(wkvt_ref[...], hb)
    kt_ref[...] = kvt[:KV_WIDTH]
    vt_ref[...] = kvt[KV_WIDTH:]
    if fox:
        logft_ref[...] = _log_sigmoid(_dot_nt(wft_ref[...], hb) + bft_ref[...])


def _proj_sample(x, g, wqg, wkvt, fox_w=None):
    n = x.shape[0]
    fox = fox_w is not None
    ins = [x, g, wqg, wkvt] + (list(fox_w) if fox else [])
    out_shape = [jax.ShapeDtypeStruct((n, BRANCH), jnp.float32), jax.ShapeDtypeStruct((n, BRANCH), jnp.float32),
                 jax.ShapeDtypeStruct((KV_WIDTH, n), jnp.float32), jax.ShapeDtypeStruct((KV_WIDTH, n), jnp.float32)]
    if fox:
        out_shape.append(jax.ShapeDtypeStruct((N_HEADS, n), jnp.float32))
    return pl.pallas_call(
        functools.partial(_proj_sample_kernel, fox=fox),
        grid=(1,),
        in_specs=[_full(a.shape) for a in ins],
        out_specs=[_full(s.shape) for s in out_shape],
        out_shape=out_shape,
        compiler_params=pltpu.CompilerParams(dimension_semantics=("arbitrary",), vmem_limit_bytes=VMEM_LIMIT),
        name="proj_sample_fox" if fox else "proj_sample_swa",
    )(*ins)


def _head_placement():
    r = np.zeros((BRANCH, KV_WIDTH), np.float32)
    for h in range(N_HEADS):
        for d in range(HEAD_DIM):
            r[h * HEAD_DIM + d, (h // GROUP) * HEAD_DIM + d] = 1.0
    return jnp.asarray(r, jnp.bfloat16), jnp.asarray(r.T, jnp.bfloat16)


def _own_lanes_mask():
    return (_iota((N_HEADS, BRANCH), 1) // HEAD_DIM) == _iota((N_HEADS, BRANCH), 0)


def _block_diag_q(qrow, r):
    dq = jnp.where(_own_lanes_mask(), jnp.broadcast_to(qrow, (N_HEADS, BRANCH)), 0.0).astype(jnp.bfloat16)
    return _dot(dq, r).astype(jnp.bfloat16)


def _gated_row(o16, gate_row, rt):
    own_kv = (_iota((N_HEADS, KV_WIDTH), 1) // HEAD_DIM) == (_iota((N_HEADS, KV_WIDTH), 0) // GROUP)
    om = jnp.where(own_kv, o16, 0.0)
    e = jnp.zeros((N_HEADS, BRANCH), jnp.float32)
    for piece in _pieces(om):
        e = e + _dot(piece, rt)
    orow = jnp.sum(jnp.where(_own_lanes_mask(), e, 0.0), axis=0, keepdims=True)
    return (orow * _silu(gate_row)).astype(jnp.bfloat16)


def _pick_lane(x, b):
    return jnp.sum(jnp.where(_iota(x.shape, 1) == b, x, 0.0), axis=1, keepdims=True)


def _swa_sample_kernel(q_ref, gate_ref, kbuf_ref, vbuf_ref, ktn_ref, vtn_ref, sink_ref, slope_ref, r_ref, rt_ref,
                       o_ref, kout_ref, vout_ref):
    b = pl.program_id(0)
    qbd = _block_diag_q(q_ref[0], r_ref[...])
    kt = kbuf_ref[0]
    vt = vbuf_ref[0]
    ktn = ktn_ref[...]
    vtn = vtn_ref[...]
    lane = _iota((N_HEADS, WINDOW), 1)
    dist = (WINDOW - lane).astype(jnp.float32)
    s_old = _dot(qbd, kt.astype(jnp.bfloat16)) - slope_ref[...] * dist
    s_new = jnp.where(lane == b, _dot(qbd, ktn.astype(jnp.bfloat16)), NEG_INF)
    sk = sink_ref[...]
    m = jnp.maximum(jnp.maximum(jnp.max(s_old, axis=1, keepdims=True), jnp.max(s_new, axis=1, keepdims=True)), sk)
    p_old = jnp.exp(s_old - m)
    p_new = jnp.exp(s_new - m)
    den = jnp.sum(p_old, axis=1, keepdims=True) + jnp.sum(p_new, axis=1, keepdims=True) + jnp.exp(sk - m)
    o16 = (_dot_nt(p_old.astype(jnp.bfloat16), vt.astype(jnp.bfloat16))
           + _dot_nt(p_new.astype(jnp.bfloat16), vtn.astype(jnp.bfloat16))) / den
    o_ref[0] = _gated_row(o16, gate_ref[0], rt_ref[...])
    last = _iota((KV_WIDTH, WINDOW), 1) == WINDOW - 1
    kout_ref[0] = jnp.where(last, _pick_lane(ktn, b), pltpu.roll(kt, WINDOW - 1, 1))
    vout_ref[0] = jnp.where(last, _pick_lane(vtn, b), pltpu.roll(vt, WINDOW - 1, 1))


def _swa_sample(q, gate, kbuf, vbuf, ktn, vtn, sink, slope, r, rt):
    n = q.shape[0]
    row = pl.BlockSpec((1, 1, BRANCH), lambda b: (b, 0, 0))
    buf = pl.BlockSpec((1, KV_WIDTH, WINDOW), lambda b: (b, 0, 0))
    return pl.pallas_call(
        _swa_sample_kernel,
        grid=(n,),
        in_specs=[row, row, buf, buf, _full(ktn.shape), _full(vtn.shape), _full(sink.shape), _full(slope.shape),
                  _full(r.shape), _full(rt.shape)],
        out_specs=[row, buf, buf],
        out_shape=[jax.ShapeDtypeStruct((n, 1, BRANCH), jnp.bfloat16),
                   jax.ShapeDtypeStruct(kbuf.shape, jnp.float32), jax.ShapeDtypeStruct(vbuf.shape, jnp.float32)],
        compiler_params=pltpu.CompilerParams(dimension_semantics=("arbitrary",), vmem_limit_bytes=VMEM_LIMIT),
        name="swa_sample",
    )(q, gate, kbuf, vbuf, ktn, vtn, sink, slope, r, rt)


def _fox_sample_kernel(pt_ref, q_ref, gate_ref, ktn_ref, vtn_ref, lfn_ref, r_ref, rt_ref, ck_hbm, cv_hbm, clf_hbm,
                       o_ref, kbuf, vbuf, lfbuf, sem, *, layer, n_pages, n_batch):
    npg = PAGES_PER_CHUNK
    n_chunks = n_pages // npg
    b = pl.program_id(0)

    def chunk_copies(row, chunk, slot):
        out = []
        for i in range(npg):
            page = pt_ref[row, n_pages - 1 - (chunk * npg + i)]
            out.append(pltpu.make_async_copy(ck_hbm.at[layer, page], kbuf.at[slot, i], sem.at[0, slot]))
            out.append(pltpu.make_async_copy(cv_hbm.at[layer, page], vbuf.at[slot, i], sem.at[1, slot]))
            out.append(pltpu.make_async_copy(clf_hbm.at[layer, page], lfbuf.at[slot, i], sem.at[2, slot]))
        return out

    @pl.when(b == 0)
    def _():
        for cp in chunk_copies(b, 0, 0):
            cp.start()

    qbd = _block_diag_q(q_ref[0], r_ref[...])
    lane = _iota((N_HEADS, LANES), 1)
    s = jnp.where(lane == b, _dot(qbd, ktn_ref[...].astype(jnp.bfloat16)), NEG_INF)
    m = jnp.max(s, axis=1, keepdims=True)
    p = jnp.exp(s - m)
    l = jnp.sum(p, axis=1, keepdims=True)
    acc = _dot_nt(p.astype(jnp.bfloat16), vtn_ref[...].astype(jnp.bfloat16))
    carry = jnp.broadcast_to(_pick_lane(lfn_ref[...], b), (N_HEADS, LANES))

    later = (_iota((PAGE, PAGE), 0) > _iota((PAGE, PAGE), 1)).astype(jnp.bfloat16)
    ones = jnp.ones((PAGE, PAGE), jnp.bfloat16)
    for chunk in range(n_chunks):
        slot = chunk % 2
        if chunk + 1 < n_chunks:
            for cp in chunk_copies(b, chunk + 1, 1 - slot):
                cp.start()
        else:
            @pl.when(b + 1 < n_batch)
            def _():
                for cp in chunk_copies(b + 1, 0, 1 - slot):
                    cp.start()
        for cp in chunk_copies(b, chunk, slot):
            cp.wait()

        lf = lfbuf[slot].reshape(npg * N_HEADS, PAGE)
        within = jnp.zeros_like(lf)
        total = jnp.zeros_like(lf)
        for piece in _pieces(lf):
            within = within + _dot(piece, later)
            total = total + _dot(piece, ones)
        scores = []
        for i in range(npg):
            rows = slice(i * N_HEADS, (i + 1) * N_HEADS)
            scores.append(_dot(qbd, kbuf[slot, i].astype(jnp.bfloat16)) + (within[rows] + carry))
            carry = carry + total[rows]
        m_cur = scores[0]
        for s in scores[1:]:
            m_cur = jnp.maximum(m_cur, s)
        m_new = jnp.maximum(m, jnp.max(m_cur, axis=1, keepdims=True))
        alpha = jnp.exp(m - m_new)
        acc = alpha * acc
        psum = jnp.zeros((N_HEADS, PAGE), jnp.float32)
        for i in range(npg):
            p = jnp.exp(scores[i] - m_new)
            psum = psum + p
            acc = acc + _dot_nt(p.astype(jnp.bfloat16), vbuf[slot, i].astype(jnp.bfloat16))
        l = alpha * l + jnp.sum(psum, axis=1, keepdims=True)
        m = m_new

    o_ref[0] = _gated_row(acc / l, gate_ref[0], rt_ref[...])


def _fox_sample(page_table, q, gate, ktn, vtn, lfn, r, rt, cache_kt, cache_vt, cache_lft, layer):
    n, n_pages = page_table.shape
    npg = PAGES_PER_CHUNK
    assert n_pages % (2 * npg) == 0
    row = pl.BlockSpec((1, 1, BRANCH), lambda b, pt: (b, 0, 0))
    const = lambda a: pl.BlockSpec(a.shape, lambda b, pt: (0,) * a.ndim)
    hbm = pl.BlockSpec(memory_space=pl.ANY)
    grid_spec = pltpu.PrefetchScalarGridSpec(
        num_scalar_prefetch=1,
        grid=(n,),
        in_specs=[row, row, const(ktn), const(vtn), const(lfn), const(r), const(rt), hbm, hbm, hbm],
        out_specs=row,
        scratch_shapes=[pltpu.VMEM((2, npg, KV_WIDTH, PAGE), jnp.float32),
                        pltpu.VMEM((2, npg, KV_WIDTH, PAGE), jnp.float32),
                        pltpu.VMEM((2, npg, N_HEADS, PAGE), jnp.float32),
                        pltpu.SemaphoreType.DMA((3, 2))])
    return pl.pallas_call(
        functools.partial(_fox_sample_kernel, layer=layer, n_pages=n_pages, n_batch=n),
        grid_spec=grid_spec,
        out_shape=jax.ShapeDtypeStruct((n, 1, BRANCH), jnp.bfloat16),
        compiler_params=pltpu.CompilerParams(dimension_semantics=("arbitrary",), vmem_limit_bytes=VMEM_LIMIT),
        name="fox_sample",
    )(page_table, q, gate, ktn, vtn, lfn, r, rt, cache_kt, cache_vt, cache_lft)


def _feature_major(t):
    lead = t.shape[:-3]
    n = len(lead)
    t = jnp.transpose(t, tuple(range(n)) + (n + 1, n + 2, n))
    return t.reshape(*lead, KV_WIDTH, t.shape[-1])


def _row_major(t):
    lead = t.shape[:-2]
    n = len(lead)
    t = t.reshape(*lead, N_KV_HEADS, HEAD_DIM, t.shape[-1])
    return jnp.transpose(t, tuple(range(n)) + (n + 2, n, n + 1))


def kernel(x_prompt, x_sample, state_swa_k, state_swa_v, cache_fox_k, cache_fox_v, cache_fox_logf, page_table,
           norm_pre, norm_post, w_in_swa, sinks_swa, w_out_swa, w_in_fox, b_forget, w_out_fox):
    B, S, _ = x_prompt.shape
    n_dec = x_sample.shape[0]
    depth = norm_pre.shape[0]
    bf16 = jnp.bfloat16
    r, rt = _head_placement()
    slope_col = jnp.asarray(np.array(_alibi_slopes(), np.float32).reshape(N_HEADS, 1))
    cache_kt = _feature_major(cache_fox_k)
    cache_vt = _feature_major(cache_fox_v)
    cache_lft = jnp.swapaxes(cache_fox_logf, -1, -2)

    xp = x_prompt
    xs = x_sample.reshape(n_dec, D_MODEL)
    swa_kp, swa_vp, swa_ks, swa_vs = [], [], [], []
    fox_kp, fox_vp, fox_fp, fox_ks, fox_vs, fox_fs = [], [], [], [], [], []
    for i in range(depth):
        j = i // 2
        fox = i % 2 == 1
        w_in = w_in_fox[j] if fox else w_in_swa[j]
        w_out = (w_out_fox[j] if fox else w_out_swa[j]).astype(bf16)
        n_main = 2 * BRANCH + 2 * KV_WIDTH
        wqg = jnp.concatenate([w_in[:, :BRANCH], w_in[:, BRANCH + 2 * KV_WIDTH:n_main]], axis=1).astype(bf16)
        wallt = w_in[:, :n_main].T.astype(bf16)
        wkvt = wallt[BRANCH:BRANCH + 2 * KV_WIDTH]
        wk = w_in[:, BRANCH:BRANCH + KV_WIDTH].astype(bf16)
        g_pre = norm_pre[i].reshape(1, D_MODEL)
        g_post = norm_post[i].reshape(1, D_MODEL)
        if fox:
            wf = w_in[:, n_main:]
            wf_pad = jnp.pad(wf, ((0, 0), (0, LANES - N_HEADS))).astype(bf16)
            wft = wf.T.astype(bf16)
            bf_row = jnp.pad(b_forget[j], (0, LANES - N_HEADS)).reshape(1, LANES)
            bf_col = b_forget[j].reshape(N_HEADS, 1)
            qt, gatet, kt, vt, ka, vt16, logft = _proj_prompt(xp, g_pre, wallt, wk, (wf_pad, wft, bf_row, bf_col))
            ogt = _fox_flash(qt, ka, vt16, gatet)
            qs, gs, ktn, vtn, lfn = _proj_sample(xs, g_pre, wqg, wkvt, (wft, bf_col))
            ogs = _fox_sample(page_table, qs.reshape(n_dec, 1, BRANCH), gs.reshape(n_dec, 1, BRANCH), ktn, vtn, lfn,
                              r, rt, cache_kt, cache_vt, cache_lft, j)
            fox_kp.append(_row_major(kt)); fox_vp.append(_row_major(vt)); fox_fp.append(jnp.swapaxes(logft, 1, 2))
            fox_ks.append(_row_major(ktn[None])[0][:, None]); fox_vs.append(_row_major(vtn[None])[0][:, None])
            fox_fs.append(lfn.T[:, None, :])
        else:
            qt, gatet, kt, vt, ka, vt16 = _proj_prompt(xp, g_pre, wallt, wk)
            ogt = _swa_prompt(sinks_swa[j], qt, ka, vt16, gatet)
            qs, gs, ktn, vtn = _proj_sample(xs, g_pre, wqg, wkvt)
            ogs, kout, vout = _swa_sample(qs.reshape(n_dec, 1, BRANCH), gs.reshape(n_dec, 1, BRANCH),
                                          _feature_major(state_swa_k[j]), _feature_major(state_swa_v[j]),
                                          ktn, vtn, sinks_swa[j].reshape(N_HEADS, 1), slope_col, r, rt)
            swa_kp.append(_row_major(kt[:, :, S - WINDOW:])); swa_vp.append(_row_major(vt[:, :, S - WINDOW:]))
            swa_ks.append(_row_major(kout)); swa_vs.append(_row_major(vout))
        xp = _out_proj_prompt(ogt, w_out, g_post, xp)
        xs = _out_proj_sample(ogs.reshape(n_dec, BRANCH), w_out, g_post, xs)
    return (xp, xs.reshape(n_dec, 1, D_MODEL),
            jnp.stack(swa_kp), jnp.stack(swa_vp), jnp.stack(swa_ks), jnp.stack(swa_vs),
            jnp.stack(fox_kp), jnp.stack(fox_vp), jnp.stack(fox_fp),
            jnp.stack(fox_ks), jnp.stack(fox_vs), jnp.stack(fox_fs))
```

```python
import functools

import numpy as np
import jax
import jax.numpy as jnp
from jax import lax
from jax.experimental import pallas as pl
from jax.experimental.pallas import tpu as pltpu

D_MODEL = 1024
N_HEADS = 16
N_KV_HEADS = 4
GROUP = N_HEADS // N_KV_HEADS
HEAD_DIM = 64
BRANCH = N_HEADS * HEAD_DIM
KV_WIDTH = N_KV_HEADS * HEAD_DIM
WINDOW = 128
PAGE = 128
RMS_EPS = 1e-6
NEG_INF = -1e30
ATT_SCALE = HEAD_DIM ** -0.5
LOG2E = 1.4426950408889634
LANES = 128
PAD_Q = N_HEADS * LANES
PAD_K = N_KV_HEADS * LANES
N_PIECES = 3
AUG_K0 = HEAD_DIM
AUG_Q0 = HEAD_DIM + N_PIECES * GROUP

PROJ_TM = 256
OUT_TM = 512
FLASH_T = 512
SWA_BLOCKS = 4
PAGES_PER_CHUNK = 16
VMEM_LIMIT = 48 * 1024 * 1024

_NT = (((1,), (1,)), ((), ()))
_TN = (((0,), (0,)), ((), ()))


def _dot(a, b):
    return jnp.dot(a, b, preferred_element_type=jnp.float32)


def _dot_nt(a, b):
    return lax.dot_general(a, b, _NT, preferred_element_type=jnp.float32)


def _pieces(x):
    hi = x.astype(jnp.bfloat16)
    r1 = x - hi.astype(jnp.float32)
    mid = r1.astype(jnp.bfloat16)
    lo = (r1 - mid.astype(jnp.float32)).astype(jnp.bfloat16)
    return hi, mid, lo


def _log_sigmoid(z):
    return -(jnp.maximum(-z, 0.0) + jnp.log1p(jnp.exp(-jnp.abs(z))))


def _silu(x):
    return x * (1.0 / (1.0 + jnp.exp(-x)))


def _rmsnorm_rows(x, g):
    ms = jnp.mean(x * x, axis=-1, keepdims=True)
    return x * lax.rsqrt(ms + RMS_EPS) * g


def _iota(shape, dim):
    return lax.broadcasted_iota(jnp.int32, shape, dim)


def _proj_prompt_kernel(*refs, fox, tm):
    if fox:
        (x_ref, g_ref, wallt_ref, wk_ref, wf_ref, wft_ref, bf_ref, bft_ref, pselk_ref, constk_ref,
         pselqt_ref, constqt_ref,
         qt_ref, gatet_ref, kt_ref, vt_ref, ka_ref, vt16_ref, logft_ref, carry_ref, carryt_ref) = refs
    else:
        (x_ref, g_ref, wallt_ref, wk_ref,
         qt_ref, gatet_ref, kt_ref, vt_ref, ka_ref, vt16_ref) = refs

    hb = _rmsnorm_rows(x_ref[0], g_ref[...]).astype(jnp.bfloat16)
    allt = _dot_nt(wallt_ref[...], hb)
    k_rows = _dot(hb, wk_ref[...])
    kt = allt[BRANCH:BRANCH + KV_WIDTH]
    vt = allt[BRANCH + KV_WIDTH:BRANCH + 2 * KV_WIDTH]
    gatet_ref[0] = allt[BRANCH + 2 * KV_WIDTH:]
    kt_ref[0] = kt
    vt_ref[0] = vt
    vt16_ref[0] = vt.astype(jnp.bfloat16)

    if fox:
        @pl.when(pl.program_id(1) == 0)
        def _():
            carry_ref[...] = jnp.zeros_like(carry_ref)
            carryt_ref[...] = jnp.zeros_like(carryt_ref)

        logf = _log_sigmoid(_dot(hb, wf_ref[...]) + bf_ref[...])
        logft = _log_sigmoid(_dot_nt(wft_ref[...], hb) + bft_ref[...])
        logft_ref[0] = logft
        row = _iota((tm, tm), 0)
        col = _iota((tm, tm), 1)
        tri = (row >= col).astype(jnp.bfloat16)
        trit = (row <= col).astype(jnp.bfloat16)
        c = carry_ref[...]
        for piece in _pieces(logf):
            c = c + _dot(tri, piece)
        carry_ref[...] = c[tm - 1:tm, :]
        ct = carryt_ref[...]
        for piece in _pieces(logft):
            ct = ct + _dot(piece, trit)
        carryt_ref[...] = carryt_ref[...] + jnp.sum(logft, axis=1, keepdims=True)
        cp =jnp.concatenate(_pieces(c * LOG2E), axis=1)
        aug_k = _dot(cp, pselk_ref[...]) + constk_ref[...]
        zpad = jnp.zeros((LANES - N_PIECES * N_HEADS, tm), jnp.bfloat16)
        cpt = jnp.concatenate(list(_pieces(ct * LOG2E)) + [zpad], axis=0)
        aug_qt = _dot(pselqt_ref[...], cpt) + constqt_ref[...]

    spare_rows = LANES - HEAD_DIM
    for h in range(N_HEADS):
        qt_ref[0, h * LANES:h * LANES + HEAD_DIM, :] = (
            allt[h * HEAD_DIM:(h + 1) * HEAD_DIM] * (ATT_SCALE * LOG2E)).astype(jnp.bfloat16)
        if fox:
            spare = aug_qt[h * spare_rows:(h + 1) * spare_rows].astype(jnp.bfloat16)
        else:
            spare = jnp.zeros((spare_rows, tm), jnp.bfloat16)
        qt_ref[0, h * LANES + HEAD_DIM:(h + 1) * LANES, :] = spare
    low = _iota((tm, LANES), 1) < HEAD_DIM
    for pair in range(N_KV_HEADS // 2):
        blk = k_rows[:, pair * LANES:(pair + 1) * LANES]
        rolled = pltpu.roll(blk, HEAD_DIM, 1)
        for kvh, val in ((2 * pair, blk), (2 * pair + 1, rolled)):
            spare = aug_k[:, kvh * LANES:(kvh + 1) * LANES] if fox else 0.0
            ka_ref[0, :, kvh * LANES:(kvh + 1) * LANES] = jnp.where(low, val, spare).astype(jnp.bfloat16)


def _aug_constants():
    spare_rows = LANES - HEAD_DIM
    pselk = np.zeros((N_PIECES * LANES, PAD_K), np.float32)
    constk = np.zeros((1, PAD_K), np.float32)
    pselqt = np.zeros((N_HEADS * spare_rows, LANES), np.float32)
    constqt = np.zeros((N_HEADS * spare_rows, 1), np.float32)
    for h in range(N_HEADS):
        kvh, g = divmod(h, GROUP)
        for j in range(N_PIECES):
            pselk[j * LANES + h, kvh * LANES + AUG_K0 + N_PIECES * g + j] = -1.0
            pselqt[h * spare_rows + AUG_Q0 - HEAD_DIM + j, j * N_HEADS + h] = 1.0
            constqt[h * spare_rows + AUG_K0 - HEAD_DIM + N_PIECES * g + j, 0] = 1.0
    for kvh in range(N_KV_HEADS):
        for j in range(N_PIECES):
            constk[0, kvh * LANES + AUG_Q0 + j] = 1.0
    return (jnp.asarray(pselk, jnp.bfloat16), jnp.asarray(constk), jnp.asarray(pselqt, jnp.bfloat16),
            jnp.asarray(constqt))


def _full(shape):
    return pl.BlockSpec(shape, lambda *_: (0,) * len(shape))


def _proj_prompt(x, g, wallt, wk, fox_w=None):
    B, S, _ = x.shape
    tm = PROJ_TM
    fox = fox_w is not None
    ins = [x, g, wallt, wk]
    in_specs = [pl.BlockSpec((1, tm, D_MODEL), lambda b, i: (b, i, 0)), _full(g.shape), _full(wallt.shape),
                _full(wk.shape)]
    feat = lambda rows: pl.BlockSpec((1, rows, tm), lambda b, i: (b, 0, i))
    out_shape = [jax.ShapeDtypeStruct((B, PAD_Q, S), jnp.bfloat16),
                 jax.ShapeDtypeStruct((B, BRANCH, S), jnp.float32),
                 jax.ShapeDtypeStruct((B, KV_WIDTH, S), jnp.float32),
                 jax.ShapeDtypeStruct((B, KV_WIDTH, S), jnp.float32),
                 jax.ShapeDtypeStruct((B, S, PAD_K), jnp.bfloat16),
                 jax.ShapeDtypeStruct((B, KV_WIDTH, S), jnp.bfloat16)]
    out_specs = [feat(PAD_Q), feat(BRANCH), feat(KV_WIDTH), feat(KV_WIDTH),
                 pl.BlockSpec((1, tm, PAD_K), lambda b, i: (b, i, 0)), feat(KV_WIDTH)]
    scratch = []
    if fox:
        extra = list(fox_w) + list(_aug_constants())
        ins += extra
        in_specs += [_full(a.shape) for a in extra]
        out_shape.append(jax.ShapeDtypeStruct((B, N_HEADS, S), jnp.float32))
        out_specs.append(feat(N_HEADS))
        scratch = [pltpu.VMEM((1, LANES), jnp.float32), pltpu.VMEM((N_HEADS, 1), jnp.float32)]
    return pl.pallas_call(
        functools.partial(_proj_prompt_kernel, fox=fox, tm=tm),
        grid=(B, S // tm),
        in_specs=in_specs,
        out_specs=out_specs,
        out_shape=out_shape,
        scratch_shapes=scratch,
        compiler_params=pltpu.CompilerParams(dimension_semantics=("arbitrary", "arbitrary"),
                                             vmem_limit_bytes=VMEM_LIMIT),
        name="proj_prompt_fox" if fox else "proj_prompt_swa",
    )(*ins)


def _alibi_slopes():
    return [float(2.0 ** (-8.0 * (h + 1) / N_HEADS)) for h in range(N_HEADS)]


def _swa_bias_table():
    r = np.arange(2 * WINDOW)[:, None]
    c = np.arange(WINDOW)[None, :]
    dist = c - r + WINDOW
    valid = (dist >= 0) & (dist <= WINDOW)
    slopes = np.array(_alibi_slopes(), np.float32)[:, None, None]
    tab = np.where(valid[None], -(slopes * dist[None].astype(np.float32)) * np.float32(LOG2E), np.float32(NEG_INF))
    tab = tab.reshape(N_KV_HEADS, GROUP, 2 * WINDOW, WINDOW).transpose(0, 2, 1, 3)
    return jnp.asarray(tab.reshape(N_KV_HEADS, 2 * WINDOW, GROUP * WINDOW), jnp.float32)


def _swa_prompt_kernel(qt_ref, kp_ref, kc_ref, vp_ref, vc_ref, gatet_ref, bias_ref, sink_ref, o_ref):
    i = pl.program_id(1)
    no_prev = (_iota((2 * WINDOW, GROUP * WINDOW), 0) < WINDOW) & (i == 0)
    for jb in range(SWA_BLOCKS):
        cols = slice(jb * WINDOW, (jb + 1) * WINDOW)
        for kvh in range(N_KV_HEADS):
            lanes = slice(kvh * LANES, (kvh + 1) * LANES)
            rows = slice(kvh * HEAD_DIM, (kvh + 1) * HEAD_DIM)
            if jb == 0:
                k_prev, v_prev = kp_ref[0, :, lanes], vp_ref[0, rows, :]
            else:
                before = slice((jb - 1) * WINDOW, jb * WINDOW)
                k_prev, v_prev = kc_ref[0, before, lanes], vc_ref[0, rows, before]
            kk = jnp.concatenate([k_prev, kc_ref[0, cols, lanes]], axis=0)
            vv = jnp.concatenate([v_prev, vc_ref[0, rows, cols]], axis=1)
            q4 = jnp.concatenate([qt_ref[0, (kvh * GROUP + g) * LANES:(kvh * GROUP + g + 1) * LANES, cols]
                                  for g in range(GROUP)], axis=1)
            bias = bias_ref[kvh]
            if jb == 0:
                bias = jnp.where(no_prev, NEG_INF, bias)
            st = _dot(kk, q4) + bias
            sk = sink_ref[kvh]
            m = jnp.maximum(jnp.max(st, axis=0, keepdims=True), sk)
            p = jnp.exp2(st - m)
            den = jnp.sum(p, axis=0, keepdims=True) + jnp.exp2(sk - m)
            ot = _dot(vv, p.astype(jnp.bfloat16)) / den
            for g in range(GROUP):
                out_rows = slice((kvh * GROUP + g) * HEAD_DIM, (kvh * GROUP + g + 1) * HEAD_DIM)
                gt = gatet_ref[0, out_rows, cols]
                o_ref[0, out_rows, cols] = (ot[:, g * WINDOW:(g + 1) * WINDOW] * _silu(gt)).astype(jnp.bfloat16)


def _swa_prompt(sink, qt, ka, vt16, gatet):
    B, _, S = qt.shape
    t = SWA_BLOCKS * WINDOW
    bias = _swa_bias_table()
    sink_lanes = jnp.repeat((sink * LOG2E).reshape(N_KV_HEADS, GROUP, 1), WINDOW, axis=2).reshape(N_KV_HEADS, 1, GROUP * WINDOW)
    feat = lambda rows: pl.BlockSpec((1, rows, t), lambda b, i: (b, 0, i))
    prev_block = lambda i: jnp.maximum(i * SWA_BLOCKS - 1, 0)
    return pl.pallas_call(
        _swa_prompt_kernel,
        grid=(B, S // t),
        in_specs=[feat(PAD_Q),
                  pl.BlockSpec((1, WINDOW, PAD_K), lambda b, i: (b, prev_block(i), 0)),
                  pl.BlockSpec((1, t, PAD_K), lambda b, i: (b, i, 0)),
                  pl.BlockSpec((1, KV_WIDTH, WINDOW), lambda b, i: (b, 0, prev_block(i))),
                  feat(KV_WIDTH),
                  feat(BRANCH),
                  _full(bias.shape), _full(sink_lanes.shape)],
        out_specs=feat(BRANCH),
        out_shape=jax.ShapeDtypeStruct((B, BRANCH, S), jnp.bfloat16),
        compiler_params=pltpu.CompilerParams(dimension_semantics=("arbitrary", "arbitrary"),
                                             vmem_limit_bytes=VMEM_LIMIT),
        name="swa_prompt",
    )(qt, ka, ka, vt16, vt16, gatet, bias, sink_lanes)


def _fox_flash_kernel(qi_ref, kj_ref, qt_ref, k_ref, v_ref, gatet_ref, o_ref, m_ref, l_ref, acc_ref, *, t):
    step = pl.program_id(2)
    qi = qi_ref[step]
    kj = kj_ref[step]

    @pl.when(kj == 0)
    def _():
        m_ref[...] = jnp.full_like(m_ref, NEG_INF)
        l_ref[...] = jnp.zeros_like(l_ref)
        acc_ref[...] = jnp.zeros_like(acc_ref)

    def update(masked):
        k = k_ref[0]
        v = v_ref[0]
        if masked:
            keep = _iota((t, t), 0) <= _iota((t, t), 1)
        score = lambda g: _dot(k, qt_ref[0, g * LANES:(g + 1) * LANES, :])

        def softmax_part(g, st):
            if masked:
                st = jnp.where(keep, st, NEG_INF)
            m_prev = m_ref[g]
            m_new = jnp.maximum(m_prev, jnp.max(st, axis=0, keepdims=True))
            alpha = jnp.exp2(m_prev - m_new)
            p = jnp.exp2(st - m_new)
            l_ref[g] = alpha * l_ref[g] + jnp.sum(p, axis=0, keepdims=True)
            m_ref[g] = m_new
            return alpha, p.astype(jnp.bfloat16)

        def value_part(g, alpha, p):
            acc_ref[g] = alpha * acc_ref[g] + _dot(v, p)

        ahead = 2
        sts = [score(g) for g in range(ahead)]
        pending = None
        for g in range(GROUP):
            if g + ahead < GROUP:
                sts.append(score(g + ahead))
            cur = softmax_part(g, sts[g])
            if pending is not None:
                value_part(g - 1, *pending)
            pending = cur
        value_part(GROUP - 1, *pending)

    @pl.when(kj < qi)
    def _():
        update(False)

    @pl.when(kj == qi)
    def _():
        update(True)
        for g in range(GROUP):
            o = acc_ref[g] / l_ref[g]
            gt = gatet_ref[0, g * HEAD_DIM:(g + 1) * HEAD_DIM, :]
            o_ref[0, g * HEAD_DIM:(g + 1) * HEAD_DIM, :] = (o * _silu(gt)).astype(jnp.bfloat16)


def _fox_flash(qt, ka, vt16, gatet):
    B, _, S = qt.shape
    t = FLASH_T
    n = S // t
    pairs = [(qi, kj) for qi in range(n) for kj in range(qi + 1)]
    qi_arr = jnp.asarray(np.array([p[0] for p in pairs], np.int32))
    kj_arr = jnp.asarray(np.array([p[1] for p in pairs], np.int32))
    gw = GROUP * HEAD_DIM
    grid_spec = pltpu.PrefetchScalarGridSpec(
        num_scalar_prefetch=2,
        grid=(B, N_KV_HEADS, len(pairs)),
        in_specs=[pl.BlockSpec((1, GROUP * LANES, t), lambda b, h, s, qi, kj: (b, h, qi[s])),
                  pl.BlockSpec((1, t, LANES), lambda b, h, s, qi, kj: (b, kj[s], h)),
                  pl.BlockSpec((1, HEAD_DIM, t), lambda b, h, s, qi, kj: (b, h, kj[s])),
                  pl.BlockSpec((1, gw, t), lambda b, h, s, qi, kj: (b, h, qi[s]))],
        out_specs=pl.BlockSpec((1, gw, t), lambda b, h, s, qi, kj: (b, h, qi[s])),
        scratch_shapes=[pltpu.VMEM((GROUP, 1, t), jnp.float32), pltpu.VMEM((GROUP, 1, t), jnp.float32),
                        pltpu.VMEM((GROUP, HEAD_DIM, t), jnp.float32)])
    return pl.pallas_call(
        functools.partial(_fox_flash_kernel, t=t),
        grid_spec=grid_spec,
        out_shape=jax.ShapeDtypeStruct((B, BRANCH, S), jnp.bfloat16),
        compiler_params=pltpu.CompilerParams(dimension_semantics=("arbitrary", "arbitrary", "arbitrary"),
                                             vmem_limit_bytes=VMEM_LIMIT),
        name="fox_flash",
    )(qi_arr, kj_arr, qt, ka, vt16, gatet)


def _out_proj_kernel(og_ref, w_ref, g_ref, x_ref, o_ref, *, feature_major):
    if feature_major:
        y = lax.dot_general(og_ref[0], w_ref[...], _TN, preferred_element_type=jnp.float32)
        o_ref[0] = x_ref[0] + _rmsnorm_rows(y, g_ref[...])
    else:
        y = _dot(og_ref[...], w_ref[...])
        o_ref[...] = x_ref[...] + _rmsnorm_rows(y, g_ref[...])


def _out_proj_prompt(ogt, w, g, x):
    B, S, _ = x.shape
    tm = OUT_TM
    rows = pl.BlockSpec((1, tm, D_MODEL), lambda b, i: (b, i, 0))
    return pl.pallas_call(
        functools.partial(_out_proj_kernel, feature_major=True),
        grid=(B, S // tm),
        in_specs=[pl.BlockSpec((1, BRANCH, tm), lambda b, i: (b, 0, i)), _full(w.shape), _full(g.shape), rows],
        out_specs=rows,
        out_shape=jax.ShapeDtypeStruct(x.shape, jnp.float32),
        compiler_params=pltpu.CompilerParams(dimension_semantics=("arbitrary", "arbitrary"),
                                             vmem_limit_bytes=VMEM_LIMIT),
        name="out_proj_prompt",
    )(ogt, w, g, x)


def _out_proj_sample(og, w, g, x):
    return pl.pallas_call(
        functools.partial(_out_proj_kernel, feature_major=False),
        grid=(1,),
        in_specs=[_full(og.shape), _full(w.shape), _full(g.shape), _full(x.shape)],
        out_specs=_full(x.shape),
        out_shape=jax.ShapeDtypeStruct(x.shape, jnp.float32),
        compiler_params=pltpu.CompilerParams(dimension_semantics=("arbitrary",), vmem_limit_bytes=VMEM_LIMIT),
        name="out_proj_sample",
    )(og, w, g, x)


def _proj_sample_kernel(*refs, fox):
    if fox:
        x_ref, g_ref, wqg_ref, wkvt_ref, wft_ref, bft_ref, q_ref, gate_ref, kt_ref, vt_ref, logft_ref = refs
    else:
        x_ref, g_ref, wqg_ref, wkvt_ref, q_ref, gate_ref, kt_ref, vt_ref = refs
    hb = _rmsnorm_rows(x_ref[...], g_ref[...]).astype(jnp.bfloat16)
    pq = _dot(hb, wqg_ref[...])
    q_ref[...] = pq[:, :BRANCH] * ATT_SCALE
    gate_ref[...] = pq[:, BRANCH:]
    kvt = _dot_nt(wkvt_ref[...], hb)
    kt_ref[...] = kvt[:KV_WIDTH]
    vt_ref[...] = kvt[KV_WIDTH:]
    if fox:
        logft_ref[...] = _log_sigmoid(_dot_nt(wft_ref[...], hb) + bft_ref[...])


def _proj_sample(x, g, wqg, wkvt, fox_w=None):
    n = x.shape[0]
    fox = fox_w is not None
    ins = [x, g, wqg, wkvt] + (list(fox_w) if fox else [])
    out_shape = [jax.ShapeDtypeStruct((n, BRANCH), jnp.float32), jax.ShapeDtypeStruct((n, BRANCH), jnp.float32),
                 jax.ShapeDtypeStruct((KV_WIDTH, n), jnp.float32), jax.ShapeDtypeStruct((KV_WIDTH, n), jnp.float32)]
    if fox:
        out_shape.append(jax.ShapeDtypeStruct((N_HEADS, n), jnp.float32))
    return pl.pallas_call(
        functools.partial(_proj_sample_kernel, fox=fox),
        grid=(1,),
        in_specs=[_full(a.shape) for a in ins],
        out_specs=[_full(s.shape) for s in out_shape],
        out_shape=out_shape,
        compiler_params=pltpu.CompilerParams(dimension_semantics=("arbitrary",), vmem_limit_bytes=VMEM_LIMIT),
        name="proj_sample_fox" if fox else "proj_sample_swa",
    )(*ins)


def _head_placement():
    r = np.zeros((BRANCH, KV_WIDTH), np.float32)
    for h in range(N_HEADS):
        for d in range(HEAD_DIM):
            r[h * HEAD_DIM + d, (h // GROUP) * HEAD_DIM + d] = 1.0
    return jnp.asarray(r, jnp.bfloat16), jnp.asarray(r.T, jnp.bfloat16)


def _own_lanes_mask():
    return (_iota((N_HEADS, BRANCH), 1) // HEAD_DIM) == _iota((N_HEADS, BRANCH), 0)


def _block_diag_q(qrow, r):
    dq = jnp.where(_own_lanes_mask(), jnp.broadcast_to(qrow, (N_HEADS, BRANCH)), 0.0).astype(jnp.bfloat16)
    return _dot(dq, r).astype(jnp.bfloat16)


def _gated_row(o16, gate_row, rt):
    own_kv = (_iota((N_HEADS, KV_WIDTH), 1) // HEAD_DIM) == (_iota((N_HEADS, KV_WIDTH), 0) // GROUP)
    om = jnp.where(own_kv, o16, 0.0)
    e = jnp.zeros((N_HEADS, BRANCH), jnp.float32)
    for piece in _pieces(om):
        e = e + _dot(piece, rt)
    orow = jnp.sum(jnp.where(_own_lanes_mask(), e, 0.0), axis=0, keepdims=True)
    return (orow * _silu(gate_row)).astype(jnp.bfloat16)


def _pick_lane(x, b):
    return jnp.sum(jnp.where(_iota(x.shape, 1) == b, x, 0.0), axis=1, keepdims=True)


def _swa_sample_kernel(q_ref, gate_ref, kbuf_ref, vbuf_ref, ktn_ref, vtn_ref, sink_ref, slope_ref, r_ref, rt_ref,
                       o_ref, kout_ref, vout_ref):
    b = pl.program_id(0)
    qbd = _block_diag_q(q_ref[0], r_ref[...])
    kt = kbuf_ref[0]
    vt = vbuf_ref[0]
    ktn = ktn_ref[...]
    vtn = vtn_ref[...]
    lane = _iota((N_HEADS, WINDOW), 1)
    dist = (WINDOW - lane).astype(jnp.float32)
    s_old = _dot(qbd, kt.astype(jnp.bfloat16)) - slope_ref[...] * dist
    s_new = jnp.where(lane == b, _dot(qbd, ktn.astype(jnp.bfloat16)), NEG_INF)
    sk = sink_ref[...]
    m = jnp.maximum(jnp.maximum(jnp.max(s_old, axis=1, keepdims=True), jnp.max(s_new, axis=1, keepdims=True)), sk)
    p_old = jnp.exp(s_old - m)
    p_new = jnp.exp(s_new - m)
    den = jnp.sum(p_old, axis=1, keepdims=True) + jnp.sum(p_new, axis=1, keepdims=True) + jnp.exp(sk - m)
    o16 = (_dot_nt(p_old.astype(jnp.bfloat16), vt.astype(jnp.bfloat16))
           + _dot_nt(p_new.astype(jnp.bfloat16), vtn.astype(jnp.bfloat16))) / den
    o_ref[0] = _gated_row(o16, gate_ref[0], rt_ref[...])
    last = _iota((KV_WIDTH, WINDOW), 1) == WINDOW - 1
    kout_ref[0] = jnp.where(last, _pick_lane(ktn, b), pltpu.roll(kt, WINDOW - 1, 1))
    vout_ref[0] = jnp.where(last, _pick_lane(vtn, b), pltpu.roll(vt, WINDOW - 1, 1))


def _swa_sample(q, gate, kbuf, vbuf, ktn, vtn, sink, slope, r, rt):
    n = q.shape[0]
    row = pl.BlockSpec((1, 1, BRANCH), lambda b: (b, 0, 0))
    buf = pl.BlockSpec((1, KV_WIDTH, WINDOW), lambda b: (b, 0, 0))
    return pl.pallas_call(
        _swa_sample_kernel,
        grid=(n,),
        in_specs=[row, row, buf, buf, _full(ktn.shape), _full(vtn.shape), _full(sink.shape), _full(slope.shape),
                  _full(r.shape), _full(rt.shape)],
        out_specs=[row, buf, buf],
        out_shape=[jax.ShapeDtypeStruct((n, 1, BRANCH), jnp.bfloat16),
                   jax.ShapeDtypeStruct(kbuf.shape, jnp.float32), jax.ShapeDtypeStruct(vbuf.shape, jnp.float32)],
        compiler_params=pltpu.CompilerParams(dimension_semantics=("arbitrary",), vmem_limit_bytes=VMEM_LIMIT),
        name="swa_sample",
    )(q, gate, kbuf, vbuf, ktn, vtn, sink, slope, r, rt)


def _fox_sample_kernel(pt_ref, q_ref, gate_ref, ktn_ref, vtn_ref, lfn_ref, r_ref, rt_ref, ck_hbm, cv_hbm, clf_hbm,
                       o_ref, kbuf, vbuf, lfbuf, sem, *, layer, n_pages, n_batch):
    npg = PAGES_PER_CHUNK
    n_chunks = n_pages // npg
    b = pl.program_id(0)

    def chunk_copies(row, chunk, slot):
        out = []
        for i in range(npg):
            page = pt_ref[row, n_pages - 1 - (chunk * npg + i)]
            out.append(pltpu.make_async_copy(ck_hbm.at[layer, page], kbuf.at[slot, i], sem.at[0, slot]))
            out.append(pltpu.make_async_copy(cv_hbm.at[layer, page], vbuf.at[slot, i], sem.at[1, slot]))
            out.append(pltpu.make_async_copy(clf_hbm.at[layer, page], lfbuf.at[slot, i], sem.at[2, slot]))
        return out

    @pl.when(b == 0)
    def _():
        for cp in chunk_copies(b, 0, 0):
            cp.start()

    qbd = _block_diag_q(q_ref[0], r_ref[...])
    lane = _iota((N_HEADS, LANES), 1)
    s = jnp.where(lane == b, _dot(qbd, ktn_ref[...].astype(jnp.bfloat16)), NEG_INF)
    m = jnp.max(s, axis=1, keepdims=True)
    p = jnp.exp(s - m)
    l = jnp.sum(p, axis=1, keepdims=True)
    acc = _dot_nt(p.astype(jnp.bfloat16), vtn_ref[...].astype(jnp.bfloat16))
    carry = jnp.broadcast_to(_pick_lane(lfn_ref[...], b), (N_HEADS, LANES))

    later = (_iota((PAGE, PAGE), 0) > _iota((PAGE, PAGE), 1)).astype(jnp.bfloat16)
    ones = jnp.ones((PAGE, PAGE), jnp.bfloat16)
    for chunk in range(n_chunks):
        slot = chunk % 2
        if chunk + 1 < n_chunks:
            for cp in chunk_copies(b, chunk + 1, 1 - slot):
                cp.start()
        else:
            @pl.when(b + 1 < n_batch)
            def _():
                for cp in chunk_copies(b + 1, 0, 1 - slot):
                    cp.start()
        for cp in chunk_copies(b, chunk, slot):
            cp.wait()

        lf = lfbuf[slot].reshape(npg * N_HEADS, PAGE)
        within = jnp.zeros_like(lf)
        total = jnp.zeros_like(lf)
        for piece in _pieces(lf):
            within = within + _dot(piece, later)
            total = total + _dot(piece, ones)
        scores = []
        for i in range(npg):
            rows = slice(i * N_HEADS, (i + 1) * N_HEADS)
            scores.append(_dot(qbd, kbuf[slot, i].astype(jnp.bfloat16)) + (within[rows] + carry))
            carry = carry + total[rows]
        m_cur = scores[0]
        for s in scores[1:]:
            m_cur = jnp.maximum(m_cur, s)
        m_new = jnp.maximum(m, jnp.max(m_cur, axis=1, keepdims=True))
        alpha = jnp.exp(m - m_new)
        acc = alpha * acc
        psum = jnp.zeros((N_HEADS, PAGE), jnp.float32)
        for i in range(npg):
            p = jnp.exp(scores[i] - m_new)
            psum = psum + p
            acc = acc + _dot_nt(p.astype(jnp.bfloat16), vbuf[slot, i].astype(jnp.bfloat16))
        l = alpha * l + jnp.sum(psum, axis=1, keepdims=True)
        m = m_new

    o_ref[0] = _gated_row(acc / l, gate_ref[0], rt_ref[...])


def _fox_sample(page_table, q, gate, ktn, vtn, lfn, r, rt, cache_kt, cache_vt, cache_lft, layer):
    n, n_pages = page_table.shape
    npg = PAGES_PER_CHUNK
    assert n_pages % (2 * npg) == 0
    row = pl.BlockSpec((1, 1, BRANCH), lambda b, pt: (b, 0, 0))
    const = lambda a: pl.BlockSpec(a.shape, lambda b, pt: (0,) * a.ndim)
    hbm = pl.BlockSpec(memory_space=pl.ANY)
    grid_spec = pltpu.PrefetchScalarGridSpec(
        num_scalar_prefetch=1,
        grid=(n,),
        in_specs=[row, row, const(ktn), const(vtn), const(lfn), const(r), const(rt), hbm, hbm, hbm],
        out_specs=row,
        scratch_shapes=[pltpu.VMEM((2, npg, KV_WIDTH, PAGE), jnp.float32),
                        pltpu.VMEM((2, npg, KV_WIDTH, PAGE), jnp.float32),
                        pltpu.VMEM((2, npg, N_HEADS, PAGE), jnp.float32),
                        pltpu.SemaphoreType.DMA((3, 2))])
    return pl.pallas_call(
        functools.partial(_fox_sample_kernel, layer=layer, n_pages=n_pages, n_batch=n),
        grid_spec=grid_spec,
        out_shape=jax.ShapeDtypeStruct((n, 1, BRANCH), jnp.bfloat16),
        compiler_params=pltpu.CompilerParams(dimension_semantics=("arbitrary",), vmem_limit_bytes=VMEM_LIMIT),
        name="fox_sample",
    )(page_table, q, gate, ktn, vtn, lfn, r, rt, cache_kt, cache_vt, cache_lft)


def _feature_major(t):
    lead = t.shape[:-3]
    n = len(lead)
    t = jnp.transpose(t, tuple(range(n)) + (n + 1, n + 2, n))
    return t.reshape(*lead, KV_WIDTH, t.shape[-1])


def _row_major(t):
    lead = t.shape[:-2]
    n = len(lead)
    t = t.reshape(*lead, N_KV_HEADS, HEAD_DIM, t.shape[-1])
    return jnp.transpose(t, tuple(range(n)) + (n + 2, n, n + 1))


def kernel(x_prompt, x_sample, state_swa_k, state_swa_v, cache_fox_k, cache_fox_v, cache_fox_logf, page_table,
           norm_pre, norm_post, w_in_swa, sinks_swa, w_out_swa, w_in_fox, b_forget, w_out_fox):
    B, S, _ = x_prompt.shape
    n_dec = x_sample.shape[0]
    depth = norm_pre.shape[0]
    bf16 = jnp.bfloat16
    r, rt = _head_placement()
    slope_col = jnp.asarray(np.array(_alibi_slopes(), np.float32).reshape(N_HEADS, 1))
    cache_kt = _feature_major(cache_fox_k)
    cache_vt = _feature_major(cache_fox_v)
    cache_lft = jnp.swapaxes(cache_fox_logf, -1, -2)

    xp = x_prompt
    xs = x_sample.reshape(n_dec, D_MODEL)
    swa_kp, swa_vp, swa_ks, swa_vs = [], [], [], []
    fox_kp, fox_vp, fox_fp, fox_ks, fox_vs, fox_fs = [], [], [], [], [], []
    for i in range(depth):
        j = i // 2
        fox = i % 2 == 1
        w_in = w_in_fox[j] if fox else w_in_swa[j]
        w_out = (w_out_fox[j] if fox else w_out_swa[j]).astype(bf16)
        n_main = 2 * BRANCH + 2 * KV_WIDTH
        wqg = jnp.concatenate([w_in[:, :BRANCH], w_in[:, BRANCH + 2 * KV_WIDTH:n_main]], axis=1).astype(bf16)
        wallt = w_in[:, :n_main].T.astype(bf16)
        wkvt = wallt[BRANCH:BRANCH + 2 * KV_WIDTH]
        wk = w_in[:, BRANCH:BRANCH + KV_WIDTH].astype(bf16)
        g_pre = norm_pre[i].reshape(1, D_MODEL)
        g_post = norm_post[i].reshape(1, D_MODEL)
        if fox:
            wf = w_in[:, n_main:]
            wf_pad = jnp.pad(wf, ((0, 0), (0, LANES - N_HEADS))).astype(bf16)
            wft = wf.T.astype(bf16)
            bf_row = jnp.pad(b_forget[j], (0, LANES - N_HEADS)).reshape(1, LANES)
            bf_col = b_forget[j].reshape(N_HEADS, 1)
            qt, gatet, kt, vt, ka, vt16, logft = _proj_prompt(xp, g_pre, wallt, wk, (wf_pad, wft, bf_row, bf_col))
            ogt = _fox_flash(qt, ka, vt16, gatet)
            qs, gs, ktn, vtn, lfn = _proj_sample(xs, g_pre, wqg, wkvt, (wft, bf_col))
            ogs = _fox_sample(page_table, qs.reshape(n_dec, 1, BRANCH), gs.reshape(n_dec, 1, BRANCH), ktn, vtn, lfn,
                              r, rt, cache_kt, cache_vt, cache_lft, j)
            fox_kp.append(_row_major(kt)); fox_vp.append(_row_major(vt)); fox_fp.append(jnp.swapaxes(logft, 1, 2))
            fox_ks.append(_row_major(ktn[None])[0][:, None]); fox_vs.append(_row_major(vtn[None])[0][:, None])
            fox_fs.append(lfn.T[:, None, :])
        else:
            qt, gatet, kt, vt, ka, vt16 = _proj_prompt(xp, g_pre, wallt, wk)
            ogt = _swa_prompt(sinks_swa[j], qt, ka, vt16, gatet)
            qs, gs, ktn, vtn = _proj_sample(xs, g_pre, wqg, wkvt)
            ogs, kout, vout = _swa_sample(qs.reshape(n_dec, 1, BRANCH), gs.reshape(n_dec, 1, BRANCH),
                                          _feature_major(state_swa_k[j]), _feature_major(state_swa_v[j]),
                                          ktn, vtn, sinks_swa[j].reshape(N_HEADS, 1), slope_col, r, rt)
            swa_kp.append(_row_major(kt[:, :, S - WINDOW:])); swa_vp.append(_row_major(vt[:, :, S - WINDOW:]))
            swa_ks.append(_row_major(kout)); swa_vs.append(_row_major(vout))
        xp = _out_proj_prompt(ogt, w_out, g_post, xp)
        xs = _out_proj_sample(ogs.reshape(n_dec, BRANCH), w_out, g_post, xs)
    return (xp, xs.reshape(n_dec, 1, D_MODEL),
            jnp.stack(swa_kp), jnp.stack(swa_vp), jnp.stack(swa_ks), jnp.stack(swa_vs),
            jnp.stack(fox_kp), jnp.stack(fox_vp), jnp.stack(fox_fp),
            jnp.stack(fox_ks), jnp.stack(fox_vs), jnp.stack(fox_fs))
```

```python
import functools

import numpy as np
import jax
import jax.numpy as jnp
from jax import lax
from jax.experimental import pallas as pl
from jax.experimental.pallas import tpu as pltpu

D_MODEL = 1024
N_HEADS = 16
N_KV_HEADS = 4
GROUP = N_HEADS // N_KV_HEADS
HEAD_DIM = 64
BRANCH = N_HEADS * HEAD_DIM
KV_WIDTH = N_KV_HEADS * HEAD_DIM
WINDOW = 128
PAGE = 128
RMS_EPS = 1e-6
NEG_INF = -1e30
ATT_SCALE = HEAD_DIM ** -0.5
LOG2E = 1.4426950408889634
LANES = 128
PAD_Q = N_HEADS * LANES
PAD_K = N_KV_HEADS * LANES
N_PIECES = 3
AUG_K0 = HEAD_DIM
AUG_Q0 = HEAD_DIM + N_PIECES * GROUP

PROJ_TM = 256
OUT_TM = 512
FLASH_T = 512
FLASH_KV_HEADS = 4
SWA_BLOCKS = 4
PAGES_PER_CHUNK = 32
VMEM_LIMIT = 48 * 1024 * 1024

_NT = (((1,), (1,)), ((), ()))
_TN = (((0,), (0,)), ((), ()))


def _dot(a, b):
    return jnp.dot(a, b, preferred_element_type=jnp.float32)


def _dot_nt(a, b):
    return lax.dot_general(a, b, _NT, preferred_element_type=jnp.float32)


def _pieces(x):
    hi = x.astype(jnp.bfloat16)
    r1 = x - hi.astype(jnp.float32)
    mid = r1.astype(jnp.bfloat16)
    lo = (r1 - mid.astype(jnp.float32)).astype(jnp.bfloat16)
    return hi, mid, lo


def _log_sigmoid(z):
    return -(jnp.maximum(-z, 0.0) + jnp.log1p(jnp.exp(-jnp.abs(z))))


def _silu(x):
    return x * (1.0 / (1.0 + jnp.exp(-x)))


def _rmsnorm_rows(x, g):
    ms = jnp.mean(x * x, axis=-1, keepdims=True)
    return x * lax.rsqrt(ms + RMS_EPS) * g


def _iota(shape, dim):
    return lax.broadcasted_iota(jnp.int32, shape, dim)


def _proj_prompt_kernel(*refs, fox, tm):
    if fox:
        (x_ref, g_ref, wallt_ref, wk_ref, wf_ref, wft_ref, bf_ref, bft_ref, pselk_ref, constk_ref,
         pselqt_ref, constqt_ref,
         qt_ref, gatet_ref, kt_ref, vt_ref, ka_ref, vt16_ref, logft_ref, carry_ref, carryt_ref) = refs
    else:
        (x_ref, g_ref, wallt_ref, wk_ref,
         qt_ref, gatet_ref, kt_ref, vt_ref, ka_ref, vt16_ref) = refs

    hb = _rmsnorm_rows(x_ref[0], g_ref[...]).astype(jnp.bfloat16)
    allt = _dot_nt(wallt_ref[...], hb)
    k_rows = _dot(hb, wk_ref[...])
    kt = allt[BRANCH:BRANCH + KV_WIDTH]
    vt = allt[BRANCH + KV_WIDTH:BRANCH + 2 * KV_WIDTH]
    gatet_ref[0] = allt[BRANCH + 2 * KV_WIDTH:]
    kt_ref[0] = kt
    vt_ref[0] = vt
    vt16_ref[0] = vt.astype(jnp.bfloat16)

    if fox:
        @pl.when(pl.program_id(1) == 0)
        def _():
            carry_ref[...] = jnp.zeros_like(carry_ref)
            carryt_ref[...] = jnp.zeros_like(carryt_ref)

        logf = _log_sigmoid(_dot(hb, wf_ref[...]) + bf_ref[...])
        logft = _log_sigmoid(_dot_nt(wft_ref[...], hb) + bft_ref[...])
        logft_ref[0] = logft
        row = _iota((tm, tm), 0)
        col = _iota((tm, tm), 1)
        tri = (row >= col).astype(jnp.bfloat16)
        trit = (row <= col).astype(jnp.bfloat16)
        c = carry_ref[...]
        for piece in _pieces(logf):
            c = c + _dot(tri, piece)
        carry_ref[...] = c[tm - 1:tm, :]
        ct = carryt_ref[...]
        for piece in _pieces(logft):
            ct = ct + _dot(piece, trit)
        carryt_ref[...] = carryt_ref[...] + jnp.sum(logft, axis=1, keepdims=True)
        cp =jnp.concatenate(_pieces(c * LOG2E), axis=1)
        aug_k = _dot(cp, pselk_ref[...]) + constk_ref[...]
        zpad = jnp.zeros((LANES - N_PIECES * N_HEADS, tm), jnp.bfloat16)
        cpt = jnp.concatenate(list(_pieces(ct * LOG2E)) + [zpad], axis=0)
        aug_qt = _dot(pselqt_ref[...], cpt) + constqt_ref[...]

    spare_rows = LANES - HEAD_DIM
    for h in range(N_HEADS):
        qt_ref[0, h * LANES:h * LANES + HEAD_DIM, :] = (
            allt[h * HEAD_DIM:(h + 1) * HEAD_DIM] * (ATT_SCALE * LOG2E)).astype(jnp.bfloat16)
        if fox:
            spare = aug_qt[h * spare_rows:(h + 1) * spare_rows].astype(jnp.bfloat16)
        else:
            spare = jnp.zeros((spare_rows, tm), jnp.bfloat16)
        qt_ref[0, h * LANES + HEAD_DIM:(h + 1) * LANES, :] = spare
    low = _iota((tm, LANES), 1) < HEAD_DIM
    for pair in range(N_KV_HEADS // 2):
        blk = k_rows[:, pair * LANES:(pair + 1) * LANES]
        rolled = pltpu.roll(blk, HEAD_DIM, 1)
        for kvh, val in ((2 * pair, blk), (2 * pair + 1, rolled)):
            spare = aug_k[:, kvh * LANES:(kvh + 1) * LANES] if fox else 0.0
            ka_ref[0, :, kvh * LANES:(kvh + 1) * LANES] = jnp.where(low, val, spare).astype(jnp.bfloat16)


def _aug_constants():
    spare_rows = LANES - HEAD_DIM
    pselk = np.zeros((N_PIECES * LANES, PAD_K), np.float32)
    constk = np.zeros((1, PAD_K), np.float32)
    pselqt = np.zeros((N_HEADS * spare_rows, LANES), np.float32)
    constqt = np.zeros((N_HEADS * spare_rows, 1), np.float32)
    for h in range(N_HEADS):
        kvh, g = divmod(h, GROUP)
        for j in range(N_PIECES):
            pselk[j * LANES + h, kvh * LANES + AUG_K0 + N_PIECES * g + j] = -1.0
            pselqt[h * spare_rows + AUG_Q0 - HEAD_DIM + j, j * N_HEADS + h] = 1.0
            constqt[h * spare_rows + AUG_K0 - HEAD_DIM + N_PIECES * g + j, 0] = 1.0
    for kvh in range(N_KV_HEADS):
        for j in range(N_PIECES):
            constk[0, kvh * LANES + AUG_Q0 + j] = 1.0
    return (jnp.asarray(pselk, jnp.bfloat16), jnp.asarray(constk), jnp.asarray(pselqt, jnp.bfloat16),
            jnp.asarray(constqt))


def _full(shape):
    return pl.BlockSpec(shape, lambda *_: (0,) * len(shape))


def _proj_prompt(x, g, wallt, wk, fox_w=None):
    B, S, _ = x.shape
    tm = PROJ_TM
    fox = fox_w is not None
    ins = [x, g, wallt, wk]
    in_specs = [pl.BlockSpec((1, tm, D_MODEL), lambda b, i: (b, i, 0)), _full(g.shape), _full(wallt.shape),
                _full(wk.shape)]
    feat = lambda rows: pl.BlockSpec((1, rows, tm), lambda b, i: (b, 0, i))
    out_shape = [jax.ShapeDtypeStruct((B, PAD_Q, S), jnp.bfloat16),
                 jax.ShapeDtypeStruct((B, BRANCH, S), jnp.float32),
                 jax.ShapeDtypeStruct((B, KV_WIDTH, S), jnp.float32),
                 jax.ShapeDtypeStruct((B, KV_WIDTH, S), jnp.float32),
                 jax.ShapeDtypeStruct((B, S, PAD_K), jnp.bfloat16),
                 jax.ShapeDtypeStruct((B, KV_WIDTH, S), jnp.bfloat16)]
    out_specs = [feat(PAD_Q), feat(BRANCH), feat(KV_WIDTH), feat(KV_WIDTH),
                 pl.BlockSpec((1, tm, PAD_K), lambda b, i: (b, i, 0)), feat(KV_WIDTH)]
    scratch = []
    if fox:
        extra = list(fox_w) + list(_aug_constants())
        ins += extra
        in_specs += [_full(a.shape) for a in extra]
        out_shape.append(jax.ShapeDtypeStruct((B, N_HEADS, S), jnp.float32))
        out_specs.append(feat(N_HEADS))
        scratch = [pltpu.VMEM((1, LANES), jnp.float32), pltpu.VMEM((N_HEADS, 1), jnp.float32)]
    return pl.pallas_call(
        functools.partial(_proj_prompt_kernel, fox=fox, tm=tm),
        grid=(B, S // tm),
        in_specs=in_specs,
        out_specs=out_specs,
        out_shape=out_shape,
        scratch_shapes=scratch,
        compiler_params=pltpu.CompilerParams(dimension_semantics=("arbitrary", "arbitrary"),
                                             vmem_limit_bytes=VMEM_LIMIT),
        name="proj_prompt_fox" if fox else "proj_prompt_swa",
    )(*ins)


def _alibi_slopes():
    return [float(2.0 ** (-8.0 * (h + 1) / N_HEADS)) for h in range(N_HEADS)]


def _swa_bias_table():
    r = np.arange(2 * WINDOW)[:, None]
    c = np.arange(WINDOW)[None, :]
    dist = c - r + WINDOW
    valid = (dist >= 0) & (dist <= WINDOW)
    slopes = np.array(_alibi_slopes(), np.float32)[:, None, None]
    tab = np.where(valid[None], -(slopes * dist[None].astype(np.float32)) * np.float32(LOG2E), np.float32(NEG_INF))
    tab = tab.reshape(N_KV_HEADS, GROUP, 2 * WINDOW, WINDOW).transpose(0, 2, 1, 3)
    return jnp.asarray(tab.reshape(N_KV_HEADS, 2 * WINDOW, GROUP * WINDOW), jnp.float32)


def _swa_prompt_kernel(qt_ref, kp_ref, kc_ref, vp_ref, vc_ref, gatet_ref, bias_ref, sink_ref, o_ref):
    i = pl.program_id(1)
    no_prev = (_iota((2 * WINDOW, GROUP * WINDOW), 0) < WINDOW) & (i == 0)
    for jb in range(SWA_BLOCKS):
        cols = slice(jb * WINDOW, (jb + 1) * WINDOW)
        for kvh in range(N_KV_HEADS):
            lanes = slice(kvh * LANES, (kvh + 1) * LANES)
            rows = slice(kvh * HEAD_DIM, (kvh + 1) * HEAD_DIM)
            if jb == 0:
                k_prev, v_prev = kp_ref[0, :, lanes], vp_ref[0, rows, :]
            else:
                before = slice((jb - 1) * WINDOW, jb * WINDOW)
                k_prev, v_prev = kc_ref[0, before, lanes], vc_ref[0, rows, before]
            kk = jnp.concatenate([k_prev, kc_ref[0, cols, lanes]], axis=0)
            vv = jnp.concatenate([v_prev, vc_ref[0, rows, cols]], axis=1)
            q4 = jnp.concatenate([qt_ref[0, (kvh * GROUP + g) * LANES:(kvh * GROUP + g + 1) * LANES, cols]
                                  for g in range(GROUP)], axis=1)
            bias = bias_ref[kvh]
            if jb == 0:
                bias = jnp.where(no_prev, NEG_INF, bias)
            st = _dot(kk, q4) + bias
            sk = sink_ref[kvh]
            m = jnp.maximum(jnp.max(st, axis=0, keepdims=True), sk)
            p = jnp.exp2(st - m)
            den = jnp.sum(p, axis=0, keepdims=True) + jnp.exp2(sk - m)
            ot = _dot(vv, p.astype(jnp.bfloat16)) / den
            for g in range(GROUP):
                out_rows = slice((kvh * GROUP + g) * HEAD_DIM, (kvh * GROUP + g + 1) * HEAD_DIM)
                gt = gatet_ref[0, out_rows, cols]
                o_ref[0, out_rows, cols] = (ot[:, g * WINDOW:(g + 1) * WINDOW] * _silu(gt)).astype(jnp.bfloat16)


def _swa_prompt(sink, qt, ka, vt16, gatet):
    B, _, S = qt.shape
    t = SWA_BLOCKS * WINDOW
    bias = _swa_bias_table()
    sink_lanes = jnp.repeat((sink * LOG2E).reshape(N_KV_HEADS, GROUP, 1), WINDOW, axis=2).reshape(N_KV_HEADS, 1, GROUP * WINDOW)
    feat = lambda rows: pl.BlockSpec((1, rows, t), lambda b, i: (b, 0, i))
    prev_block = lambda i: jnp.maximum(i * SWA_BLOCKS - 1, 0)
    return pl.pallas_call(
        _swa_prompt_kernel,
        grid=(B, S // t),
        in_specs=[feat(PAD_Q),
                  pl.BlockSpec((1, WINDOW, PAD_K), lambda b, i: (b, prev_block(i), 0)),
                  pl.BlockSpec((1, t, PAD_K), lambda b, i: (b, i, 0)),
                  pl.BlockSpec((1, KV_WIDTH, WINDOW), lambda b, i: (b, 0, prev_block(i))),
                  feat(KV_WIDTH),
                  feat(BRANCH),
                  _full(bias.shape), _full(sink_lanes.shape)],
        out_specs=feat(BRANCH),
        out_shape=jax.ShapeDtypeStruct((B, BRANCH, S), jnp.bfloat16),
        compiler_params=pltpu.CompilerParams(dimension_semantics=("arbitrary", "arbitrary"),
                                             vmem_limit_bytes=VMEM_LIMIT),
        name="swa_prompt",
    )(qt, ka, ka, vt16, vt16, gatet, bias, sink_lanes)


def _fox_flash_kernel(qi_ref, kj_ref, qt_ref, k_ref, v_ref, gatet_ref, o_ref, m_ref, l_ref, acc_ref, *, t):
    n_heads = FLASH_KV_HEADS * GROUP
    step = pl.program_id(2)
    qi = qi_ref[step]
    kj = kj_ref[step]

    @pl.when(kj == 0)
    def _():
        m_ref[...] = jnp.full_like(m_ref, NEG_INF)
        l_ref[...] = jnp.zeros_like(l_ref)
        acc_ref[...] = jnp.zeros_like(acc_ref)

    def update(masked):
        ks = [k_ref[0, :, j * LANES:(j + 1) * LANES] for j in range(FLASH_KV_HEADS)]
        vs = [v_ref[0, j * HEAD_DIM:(j + 1) * HEAD_DIM, :] for j in range(FLASH_KV_HEADS)]
        if masked:
            keep = _iota((t, t), 0) <= _iota((t, t), 1)
        score = lambda g: _dot(ks[g // GROUP], qt_ref[0, g * LANES:(g + 1) * LANES, :])

        def softmax_part(g, st):
            if masked:
                st = jnp.where(keep, st, NEG_INF)
            m_prev = m_ref[g]
            m_new = jnp.maximum(m_prev, jnp.max(st, axis=0, keepdims=True))
            alpha = jnp.exp2(m_prev - m_new)
            p = jnp.exp2(st - m_new)
            l_ref[g] = alpha * l_ref[g] + jnp.sum(p, axis=0, keepdims=True)
            m_ref[g] = m_new
            return alpha, p.astype(jnp.bfloat16)

        def value_part(g, alpha, p):
            acc_ref[g] = alpha * acc_ref[g] + _dot(vs[g // GROUP], p)

        ahead = 2
        sts = [score(g) for g in range(ahead)]
        pending = None
        for g in range(n_heads):
            if g + ahead < n_heads:
                sts.append(score(g + ahead))
            cur = softmax_part(g, sts[g])
            sts[g] = None
            if pending is not None:
                value_part(g - 1, *pending)
            pending = cur
        value_part(n_heads - 1, *pending)

    @pl.when(kj < qi)
    def _():
        update(False)

    @pl.when(kj == qi)
    def _():
        update(True)
        for g in range(n_heads):
            o = acc_ref[g] / l_ref[g]
            gt = gatet_ref[0, g * HEAD_DIM:(g + 1) * HEAD_DIM, :]
            o_ref[0, g * HEAD_DIM:(g + 1) * HEAD_DIM, :] = (o * _silu(gt)).astype(jnp.bfloat16)


def _fox_flash(qt, ka, vt16, gatet):
    B, _, S = qt.shape
    t = FLASH_T
    n = S // t
    nkv = FLASH_KV_HEADS
    n_heads = nkv * GROUP
    pairs = [(qi, kj) for qi in range(n) for kj in range(qi + 1)]
    qi_arr = jnp.asarray(np.array([p[0] for p in pairs], np.int32))
    kj_arr = jnp.asarray(np.array([p[1] for p in pairs], np.int32))
    gw = n_heads * HEAD_DIM
    grid_spec = pltpu.PrefetchScalarGridSpec(
        num_scalar_prefetch=2,
        grid=(B, N_KV_HEADS // nkv, len(pairs)),
        in_specs=[pl.BlockSpec((1, n_heads * LANES, t), lambda b, h, s, qi, kj: (b, h, qi[s])),
                  pl.BlockSpec((1, t, nkv * LANES), lambda b, h, s, qi, kj: (b, kj[s], h)),
                  pl.BlockSpec((1, nkv * HEAD_DIM, t), lambda b, h, s, qi, kj: (b, h, kj[s])),
                  pl.BlockSpec((1, gw, t), lambda b, h, s, qi, kj: (b, h, qi[s]))],
        out_specs=pl.BlockSpec((1, gw, t), lambda b, h, s, qi, kj: (b, h, qi[s])),
        scratch_shapes=[pltpu.VMEM((n_heads, 1, t), jnp.float32), pltpu.VMEM((n_heads, 1, t), jnp.float32),
                        pltpu.VMEM((n_heads, HEAD_DIM, t), jnp.float32)])
    return pl.pallas_call(
        functools.partial(_fox_flash_kernel, t=t),
        grid_spec=grid_spec,
        out_shape=jax.ShapeDtypeStruct((B, BRANCH, S), jnp.bfloat16),
        compiler_params=pltpu.CompilerParams(dimension_semantics=("arbitrary", "arbitrary", "arbitrary"),
                                             vmem_limit_bytes=VMEM_LIMIT),
        name="fox_flash",
    )(qi_arr, kj_arr, qt, ka, vt16, gatet)


def _out_proj_kernel(og_ref, w_ref, g_ref, x_ref, o_ref, *, feature_major):
    if feature_major:
        y = lax.dot_general(og_ref[0], w_ref[...], _TN, preferred_element_type=jnp.float32)
        o_ref[0] = x_ref[0] + _rmsnorm_rows(y, g_ref[...])
    else:
        y = _dot(og_ref[...], w_ref[...])
        o_ref[...] = x_ref[...] + _rmsnorm_rows(y, g_ref[...])


def _out_proj_prompt(ogt, w, g, x):
    B, S, _ = x.shape
    tm = OUT_TM
    rows = pl.BlockSpec((1, tm, D_MODEL), lambda b, i: (b, i, 0))
    return pl.pallas_call(
        functools.partial(_out_proj_kernel, feature_major=True),
        grid=(B, S // tm),
        in_specs=[pl.BlockSpec((1, BRANCH, tm), lambda b, i: (b, 0, i)), _full(w.shape), _full(g.shape), rows],
        out_specs=rows,
        out_shape=jax.ShapeDtypeStruct(x.shape, jnp.float32),
        compiler_params=pltpu.CompilerParams(dimension_semantics=("arbitrary", "arbitrary"),
                                             vmem_limit_bytes=VMEM_LIMIT),
        name="out_proj_prompt",
    )(ogt, w, g, x)


def _out_proj_sample(og, w, g, x):
    return pl.pallas_call(
        functools.partial(_out_proj_kernel, feature_major=False),
        grid=(1,),
        in_specs=[_full(og.shape), _full(w.shape), _full(g.shape), _full(x.shape)],
        out_specs=_full(x.shape),
        out_shape=jax.ShapeDtypeStruct(x.shape, jnp.float32),
        compiler_params=pltpu.CompilerParams(dimension_semantics=("arbitrary",), vmem_limit_bytes=VMEM_LIMIT),
        name="out_proj_sample",
    )(og, w, g, x)


def _proj_sample_kernel(*refs, fox):
    if fox:
        x_ref, g_ref, wqg_ref, wkvt_ref, wft_ref, bft_ref, q_ref, gate_ref, kt_ref, vt_ref, logft_ref = refs
    else:
        x_ref, g_ref, wqg_ref, wkvt_ref, q_ref, gate_ref, kt_ref, vt_ref = refs
    hb = _rmsnorm_rows(x_ref[...], g_ref[...]).astype(jnp.bfloat16)
    pq = _dot(hb, wqg_ref[...])
    q_ref[...] = pq[:, :BRANCH] * ATT_SCALE
    gate_ref[...] = pq[:, BRANCH:]
    kvt = _dot_nt(wkvt_ref[...], hb)
    kt_ref[...] = kvt[:KV_WIDTH]
    vt_ref[...] = kvt[KV_WIDTH:]
    if fox:
        logft_ref[...] = _log_sigmoid(_dot_nt(wft_ref[...], hb) + bft_ref[...])


def _proj_sample(x, g, wqg, wkvt, fox_w=None):
    n = x.shape[0]
    fox = fox_w is not None
    ins = [x, g, wqg, wkvt] + (list(fox_w) if fox else [])
    out_shape = [jax.ShapeDtypeStruct((n, BRANCH), jnp.float32), jax.ShapeDtypeStruct((n, BRANCH), jnp.float32),
                 jax.ShapeDtypeStruct((KV_WIDTH, n), jnp.float32), jax.ShapeDtypeStruct((KV_WIDTH, n), jnp.float32)]
    if fox:
        out_shape.append(jax.ShapeDtypeStruct((N_HEADS, n), jnp.float32))
    return pl.pallas_call(
        functools.partial(_proj_sample_kernel, fox=fox),
        grid=(1,),
        in_specs=[_full(a.shape) for a in ins],
        out_specs=[_full(s.shape) for s in out_shape],
        out_shape=out_shape,
        compiler_params=pltpu.CompilerParams(dimension_semantics=("arbitrary",), vmem_limit_bytes=VMEM_LIMIT),
        name="proj_sample_fox" if fox else "proj_sample_swa",
    )(*ins)


def _head_placement():
    r = np.zeros((BRANCH, KV_WIDTH), np.float32)
    for h in range(N_HEADS):
        for d in range(HEAD_DIM):
            r[h * HEAD_DIM + d, (h // GROUP) * HEAD_DIM + d] = 1.0
    return jnp.asarray(r, jnp.bfloat16), jnp.asarray(r.T, jnp.bfloat16)


def _own_lanes_mask():
    return (_iota((N_HEADS, BRANCH), 1) // HEAD_DIM) == _iota((N_HEADS, BRANCH), 0)


def _block_diag_q(qrow, r):
    dq = jnp.where(_own_lanes_mask(), jnp.broadcast_to(qrow, (N_HEADS, BRANCH)), 0.0).astype(jnp.bfloat16)
    return _dot(dq, r).astype(jnp.bfloat16)


def _gated_row(o16, gate_row, rt):
    own_kv = (_iota((N_HEADS, KV_WIDTH), 1) // HEAD_DIM) == (_iota((N_HEADS, KV_WIDTH), 0) // GROUP)
    om = jnp.where(own_kv, o16, 0.0)
    e = jnp.zeros((N_HEADS, BRANCH), jnp.float32)
    for piece in _pieces(om):
        e = e + _dot(piece, rt)
    orow = jnp.sum(jnp.where(_own_lanes_mask(), e, 0.0), axis=0, keepdims=True)
    return (orow * _silu(gate_row)).astype(jnp.bfloat16)


def _pick_lane(x, b):
    return jnp.sum(jnp.where(_iota(x.shape, 1) == b, x, 0.0), axis=1, keepdims=True)


def _swa_sample_kernel(q_ref, gate_ref, kbuf_ref, vbuf_ref, ktn_ref, vtn_ref, sink_ref, slope_ref, r_ref, rt_ref,
                       o_ref, kout_ref, vout_ref):
    b = pl.program_id(0)
    qbd = _block_diag_q(q_ref[0], r_ref[...])
    kt = kbuf_ref[0]
    vt = vbuf_ref[0]
    ktn = ktn_ref[...]
    vtn = vtn_ref[...]
    lane = _iota((N_HEADS, WINDOW), 1)
    dist = (WINDOW - lane).astype(jnp.float32)
    s_old = _dot(qbd, kt.astype(jnp.bfloat16)) - slope_ref[...] * dist
    s_new = jnp.where(lane == b, _dot(qbd, ktn.astype(jnp.bfloat16)), NEG_INF)
    sk = sink_ref[...]
    m = jnp.maximum(jnp.maximum(jnp.max(s_old, axis=1, keepdims=True), jnp.max(s_new, axis=1, keepdims=True)), sk)
    p_old = jnp.exp(s_old - m)
    p_new = jnp.exp(s_new - m)
    den = jnp.sum(p_old, axis=1, keepdims=True) + jnp.sum(p_new, axis=1, keepdims=True) + jnp.exp(sk - m)
    o16 = (_dot_nt(p_old.astype(jnp.bfloat16), vt.astype(jnp.bfloat16))
           + _dot_nt(p_new.astype(jnp.bfloat16), vtn.astype(jnp.bfloat16))) / den
    o_ref[0] = _gated_row(o16, gate_ref[0], rt_ref[...])
    last = _iota((KV_WIDTH, WINDOW), 1) == WINDOW - 1
    kout_ref[0] = jnp.where(last, _pick_lane(ktn, b), pltpu.roll(kt, WINDOW - 1, 1))
    vout_ref[0] = jnp.where(last, _pick_lane(vtn, b), pltpu.roll(vt, WINDOW - 1, 1))


def _swa_sample(q, gate, kbuf, vbuf, ktn, vtn, sink, slope, r, rt):
    n = q.shape[0]
    row = pl.BlockSpec((1, 1, BRANCH), lambda b: (b, 0, 0))
    buf = pl.BlockSpec((1, KV_WIDTH, WINDOW), lambda b: (b, 0, 0))
    return pl.pallas_call(
        _swa_sample_kernel,
        grid=(n,),
        in_specs=[row, row, buf, buf, _full(ktn.shape), _full(vtn.shape), _full(sink.shape), _full(slope.shape),
                  _full(r.shape), _full(rt.shape)],
        out_specs=[row, buf, buf],
        out_shape=[jax.ShapeDtypeStruct((n, 1, BRANCH), jnp.bfloat16),
                   jax.ShapeDtypeStruct(kbuf.shape, jnp.float32), jax.ShapeDtypeStruct(vbuf.shape, jnp.float32)],
        compiler_params=pltpu.CompilerParams(dimension_semantics=("arbitrary",), vmem_limit_bytes=VMEM_LIMIT),
        name="swa_sample",
    )(q, gate, kbuf, vbuf, ktn, vtn, sink, slope, r, rt)


def _fox_sample_kernel(pt_ref, q_ref, gate_ref, ktn_ref, vtn_ref, lfn_ref, r_ref, rt_ref, ck_hbm, cv_hbm, clf_hbm,
                       o_ref, kbuf, vbuf, lfbuf, sem, *, layer, n_pages, n_batch):
    npg = PAGES_PER_CHUNK
    n_chunks = n_pages // npg
    b = pl.program_id(0)

    def chunk_copies(row, chunk, slot):
        out = []
        for i in range(npg):
            page = pt_ref[row, n_pages - 1 - (chunk * npg + i)]
            out.append((pltpu.make_async_copy(ck_hbm.at[layer, page], kbuf.at[slot, i], sem.at[0, slot]),
                        pltpu.make_async_copy(cv_hbm.at[layer, page], vbuf.at[slot, i], sem.at[1, slot]),
                        pltpu.make_async_copy(clf_hbm.at[layer, page], lfbuf.at[slot, i], sem.at[2, slot])))
        return out

    def start_chunk(row, chunk, slot):
        for page_copies in chunk_copies(row, chunk, slot):
            for cp in page_copies:
                cp.start()

    @pl.when(b == 0)
    def _():
        start_chunk(b, 0, 0)

    qbd = _block_diag_q(q_ref[0], r_ref[...])
    lane = _iota((N_HEADS, LANES), 1)
    s = jnp.where(lane == b, _dot(qbd, ktn_ref[...].astype(jnp.bfloat16)), NEG_INF)
    m = jnp.max(s, axis=1, keepdims=True)
    p = jnp.exp(s - m)
    l = jnp.sum(p, axis=1, keepdims=True)
    acc = _dot_nt(p.astype(jnp.bfloat16), vtn_ref[...].astype(jnp.bfloat16))
    carry = jnp.broadcast_to(_pick_lane(lfn_ref[...], b), (N_HEADS, LANES))

    later = (_iota((PAGE, PAGE), 0) > _iota((PAGE, PAGE), 1)).astype(jnp.bfloat16)
    ones = jnp.ones((PAGE, PAGE), jnp.bfloat16)
    for chunk in range(n_chunks):
        slot = chunk % 2
        if chunk + 1 < n_chunks:
            start_chunk(b, chunk + 1, 1 - slot)
        else:
            @pl.when(b + 1 < n_batch)
            def _():
                start_chunk(b + 1, 0, 1 - slot)
        arriving = chunk_copies(b, chunk, slot)
        for k_copy, _, lf_copy in arriving:
            lf_copy.wait()
            k_copy.wait()

        lf = lfbuf[slot].reshape(npg * N_HEADS, PAGE)
        within = jnp.zeros_like(lf)
        total = jnp.zeros_like(lf)
        for piece in _pieces(lf):
            within = within + _dot(piece, later)
            total = total + _dot(piece, ones)
        scores = []
        for i in range(npg):
            rows = slice(i * N_HEADS, (i + 1) * N_HEADS)
            scores.append(_dot(qbd, kbuf[slot, i].astype(jnp.bfloat16)) + (within[rows] + carry))
            carry = carry + total[rows]
        m_cur = scores[0]
        for s in scores[1:]:
            m_cur = jnp.maximum(m_cur, s)
        m_new = jnp.maximum(m, jnp.max(m_cur, axis=1, keepdims=True))
        alpha = jnp.exp(m - m_new)
        for _, v_copy, _ in arriving:
            v_copy.wait()
        parts = [alpha * acc, jnp.zeros_like(acc)]
        psum = jnp.zeros((N_HEADS, PAGE), jnp.float32)
        for i in range(npg):
            p = jnp.exp(scores[i] - m_new)
            psum = psum + p
            parts[i % 2] = parts[i % 2] + _dot_nt(p.astype(jnp.bfloat16), vbuf[slot, i].astype(jnp.bfloat16))
        acc = parts[0] + parts[1]
        l = alpha * l + jnp.sum(psum, axis=1, keepdims=True)
        m = m_new

    o_ref[0] = _gated_row(acc / l, gate_ref[0], rt_ref[...])


def _fox_sample(page_table, q, gate, ktn, vtn, lfn, r, rt, cache_kt, cache_vt, cache_lft, layer):
    n, n_pages = page_table.shape
    npg = PAGES_PER_CHUNK
    assert n_pages % (2 * npg) == 0
    row = pl.BlockSpec((1, 1, BRANCH), lambda b, pt: (b, 0, 0))
    const = lambda a: pl.BlockSpec(a.shape, lambda b, pt: (0,) * a.ndim)
    hbm = pl.BlockSpec(memory_space=pl.ANY)
    grid_spec = pltpu.PrefetchScalarGridSpec(
        num_scalar_prefetch=1,
        grid=(n,),
        in_specs=[row, row, const(ktn), const(vtn), const(lfn), const(r), const(rt), hbm, hbm, hbm],
        out_specs=row,
        scratch_shapes=[pltpu.VMEM((2, npg, KV_WIDTH, PAGE), jnp.float32),
                        pltpu.VMEM((2, npg, KV_WIDTH, PAGE), jnp.float32),
                        pltpu.VMEM((2, npg, N_HEADS, PAGE), jnp.float32),
                        pltpu.SemaphoreType.DMA((3, 2))])
    return pl.pallas_call(
        functools.partial(_fox_sample_kernel, layer=layer, n_pages=n_pages, n_batch=n),
        grid_spec=grid_spec,
        out_shape=jax.ShapeDtypeStruct((n, 1, BRANCH), jnp.bfloat16),
        compiler_params=pltpu.CompilerParams(dimension_semantics=("arbitrary",), vmem_limit_bytes=VMEM_LIMIT),
        name="fox_sample",
    )(page_table, q, gate, ktn, vtn, lfn, r, rt, cache_kt, cache_vt, cache_lft)


def _feature_major(t):
    lead = t.shape[:-3]
    n = len(lead)
    t = jnp.transpose(t, tuple(range(n)) + (n + 1, n + 2, n))
    return t.reshape(*lead, KV_WIDTH, t.shape[-1])


def _row_major(t):
    lead = t.shape[:-2]
    n = len(lead)
    t = t.reshape(*lead, N_KV_HEADS, HEAD_DIM, t.shape[-1])
    return jnp.transpose(t, tuple(range(n)) + (n + 2, n, n + 1))


def kernel(x_prompt, x_sample, state_swa_k, state_swa_v, cache_fox_k, cache_fox_v, cache_fox_logf, page_table,
           norm_pre, norm_post, w_in_swa, sinks_swa, w_out_swa, w_in_fox, b_forget, w_out_fox):
    B, S, _ = x_prompt.shape
    n_dec = x_sample.shape[0]
    depth = norm_pre.shape[0]
    bf16 = jnp.bfloat16
    r, rt = _head_placement()
    slope_col = jnp.asarray(np.array(_alibi_slopes(), np.float32).reshape(N_HEADS, 1))
    cache_kt = _feature_major(cache_fox_k)
    cache_vt = _feature_major(cache_fox_v)
    cache_lft = jnp.swapaxes(cache_fox_logf, -1, -2)

    xp = x_prompt
    xs = x_sample.reshape(n_dec, D_MODEL)
    swa_kp, swa_vp, swa_ks, swa_vs = [], [], [], []
    fox_kp, fox_vp, fox_fp, fox_ks, fox_vs, fox_fs = [], [], [], [], [], []
    for i in range(depth):
        j = i // 2
        fox = i % 2 == 1
        w_in = w_in_fox[j] if fox else w_in_swa[j]
        w_out = (w_out_fox[j] if fox else w_out_swa[j]).astype(bf16)
        n_main = 2 * BRANCH + 2 * KV_WIDTH
        wqg = jnp.concatenate([w_in[:, :BRANCH], w_in[:, BRANCH + 2 * KV_WIDTH:n_main]], axis=1).astype(bf16)
        wallt = w_in[:, :n_main].T.astype(bf16)
        wkvt = wallt[BRANCH:BRANCH + 2 * KV_WIDTH]
        wk = w_in[:, BRANCH:BRANCH + KV_WIDTH].astype(bf16)
        g_pre = norm_pre[i].reshape(1, D_MODEL)
        g_post = norm_post[i].reshape(1, D_MODEL)
        if fox:
            wf = w_in[:, n_main:]
            wf_pad = jnp.pad(wf, ((0, 0), (0, LANES - N_HEADS))).astype(bf16)
            wft = wf.T.astype(bf16)
            bf_row = jnp.pad(b_forget[j], (0, LANES - N_HEADS)).reshape(1, LANES)
            bf_col = b_forget[j].reshape(N_HEADS, 1)
            qt, gatet, kt, vt, ka, vt16, logft = _proj_prompt(xp, g_pre, wallt, wk, (wf_pad, wft, bf_row, bf_col))
            ogt = _fox_flash(qt, ka, vt16, gatet)
            qs, gs, ktn, vtn, lfn = _proj_sample(xs, g_pre, wqg, wkvt, (wft, bf_col))
            ogs = _fox_sample(page_table, qs.reshape(n_dec, 1, BRANCH), gs.reshape(n_dec, 1, BRANCH), ktn, vtn, lfn,
                              r, rt, cache_kt, cache_vt, cache_lft, j)
            fox_kp.append(_row_major(kt)); fox_vp.append(_row_major(vt)); fox_fp.append(jnp.swapaxes(logft, 1, 2))
            fox_ks.append(_row_major(ktn[None])[0][:, None]); fox_vs.append(_row_major(vtn[None])[0][:, None])
            fox_fs.append(lfn.T[:, None, :])
        else:
            qt, gatet, kt, vt, ka, vt16 = _proj_prompt(xp, g_pre, wallt, wk)
            ogt = _swa_prompt(sinks_swa[j], qt, ka, vt16, gatet)
            qs, gs, ktn, vtn = _proj_sample(xs, g_pre, wqg, wkvt)
            ogs, kout, vout = _swa_sample(qs.reshape(n_dec, 1, BRANCH), gs.reshape(n_dec, 1, BRANCH),
                                          _feature_major(state_swa_k[j]), _feature_major(state_swa_v[j]),
                                          ktn, vtn, sinks_swa[j].reshape(N_HEADS, 1), slope_col, r, rt)
            swa_kp.append(_row_major(kt[:, :, S - WINDOW:])); swa_vp.append(_row_major(vt[:, :, S - WINDOW:]))
            swa_ks.append(_row_major(kout)); swa_vs.append(_row_major(vout))
        xp = _out_proj_prompt(ogt, w_out, g_post, xp)
        xs = _out_proj_sample(ogs.reshape(n_dec, BRANCH), w_out, g_post, xs)
    return (xp, xs.reshape(n_dec, 1, D_MODEL),
            jnp.stack(swa_kp), jnp.stack(swa_vp), jnp.stack(swa_ks), jnp.stack(swa_vs),
            jnp.stack(fox_kp), jnp.stack(fox_vp), jnp.stack(fox_fp),
            jnp.stack(fox_ks), jnp.stack(fox_vs), jnp.stack(fox_fs))
```

```python
import functools

import numpy as np
import jax
import jax.numpy as jnp
from jax import lax
from jax.experimental import pallas as pl
from jax.experimental.pallas import tpu as pltpu

D_MODEL = 1024
N_HEADS = 16
N_KV_HEADS = 4
GROUP = N_HEADS // N_KV_HEADS
HEAD_DIM = 64
BRANCH = N_HEADS * HEAD_DIM
KV_WIDTH = N_KV_HEADS * HEAD_DIM
WINDOW = 128
PAGE = 128
RMS_EPS = 1e-6
NEG_INF = -1e30
ATT_SCALE = HEAD_DIM ** -0.5
LOG2E = 1.4426950408889634
LANES = 128
PAD_Q = N_HEADS * LANES
PAD_K = N_KV_HEADS * LANES
N_PIECES = 3
AUG_K0 = HEAD_DIM
AUG_Q0 = HEAD_DIM + N_PIECES * GROUP

PROJ_TM = 256
OUT_TM = 512
FLASH_T = 512
FLASH_KV_HEADS = 4
SWA_BLOCKS = 4
PAGES_PER_CHUNK = 32
VMEM_LIMIT = 48 * 1024 * 1024

_NT = (((1,), (1,)), ((), ()))
_TN = (((0,), (0,)), ((), ()))


def _dot(a, b):
    return jnp.dot(a, b, preferred_element_type=jnp.float32)


def _dot_nt(a, b):
    return lax.dot_general(a, b, _NT, preferred_element_type=jnp.float32)


def _pieces(x):
    hi = x.astype(jnp.bfloat16)
    r1 = x - hi.astype(jnp.float32)
    mid = r1.astype(jnp.bfloat16)
    lo = (r1 - mid.astype(jnp.float32)).astype(jnp.bfloat16)
    return hi, mid, lo


def _log_sigmoid(z):
    return -(jnp.maximum(-z, 0.0) + jnp.log1p(jnp.exp(-jnp.abs(z))))


def _silu(x):
    return x * (1.0 / (1.0 + jnp.exp(-x)))


def _rmsnorm_rows(x, g):
    ms = jnp.mean(x * x, axis=-1, keepdims=True)
    return x * lax.rsqrt(ms + RMS_EPS) * g


def _iota(shape, dim):
    return lax.broadcasted_iota(jnp.int32, shape, dim)


def _proj_prompt_kernel(*refs, fox, tm):
    if fox:
        (x_ref, g_ref, wallt_ref, wk_ref, wf_ref, wft_ref, bf_ref, bft_ref, pselk_ref, constk_ref,
         pselqt_ref, constqt_ref,
         qt_ref, gatet_ref, kt_ref, vt_ref, ka_ref, vt16_ref, logft_ref, carry_ref, carryt_ref) = refs
    else:
        (x_ref, g_ref, wallt_ref, wk_ref,
         qt_ref, gatet_ref, kt_ref, vt_ref, ka_ref, vt16_ref) = refs

    hb = _rmsnorm_rows(x_ref[0], g_ref[...]).astype(jnp.bfloat16)
    allt = _dot_nt(wallt_ref[...], hb)
    k_rows = _dot(hb, wk_ref[...])
    kt = allt[BRANCH:BRANCH + KV_WIDTH]
    vt = allt[BRANCH + KV_WIDTH:BRANCH + 2 * KV_WIDTH]
    gatet_ref[0] = allt[BRANCH + 2 * KV_WIDTH:]
    kt_ref[0] = kt
    vt_ref[0] = vt
    vt16_ref[0] = vt.astype(jnp.bfloat16)

    if fox:
        @pl.when(pl.program_id(1) == 0)
        def _():
            carry_ref[...] = jnp.zeros_like(carry_ref)
            carryt_ref[...] = jnp.zeros_like(carryt_ref)

        logf = _log_sigmoid(_dot(hb, wf_ref[...]) + bf_ref[...])
        logft = _log_sigmoid(_dot_nt(wft_ref[...], hb) + bft_ref[...])
        logft_ref[0] = logft
        row = _iota((tm, tm), 0)
        col = _iota((tm, tm), 1)
        tri = (row >= col).astype(jnp.bfloat16)
        trit = (row <= col).astype(jnp.bfloat16)
        c = carry_ref[...]
        for piece in _pieces(logf):
            c = c + _dot(tri, piece)
        carry_ref[...] = c[tm - 1:tm, :]
        ct = carryt_ref[...]
        for piece in _pieces(logft):
            ct = ct + _dot(piece, trit)
        carryt_ref[...] = carryt_ref[...] + jnp.sum(logft, axis=1, keepdims=True)
        cp =jnp.concatenate(_pieces(c * LOG2E), axis=1)
        aug_k = _dot(cp, pselk_ref[...]) + constk_ref[...]
        zpad = jnp.zeros((LANES - N_PIECES * N_HEADS, tm), jnp.bfloat16)
        cpt = jnp.concatenate(list(_pieces(ct * LOG2E)) + [zpad], axis=0)
        aug_qt = _dot(pselqt_ref[...], cpt) + constqt_ref[...]

    spare_rows = LANES - HEAD_DIM
    for h in range(N_HEADS):
        qt_ref[0, h * LANES:h * LANES + HEAD_DIM, :] = (
            allt[h * HEAD_DIM:(h + 1) * HEAD_DIM] * (ATT_SCALE * LOG2E)).astype(jnp.bfloat16)
        if fox:
            spare = aug_qt[h * spare_rows:(h + 1) * spare_rows].astype(jnp.bfloat16)
        else:
            spare = jnp.zeros((spare_rows, tm), jnp.bfloat16)
        qt_ref[0, h * LANES + HEAD_DIM:(h + 1) * LANES, :] = spare
    low = _iota((tm, LANES), 1) < HEAD_DIM
    for pair in range(N_KV_HEADS // 2):
        blk = k_rows[:, pair * LANES:(pair + 1) * LANES]
        rolled = pltpu.roll(blk, HEAD_DIM, 1)
        for kvh, val in ((2 * pair, blk), (2 * pair + 1, rolled)):
            spare = aug_k[:, kvh * LANES:(kvh + 1) * LANES] if fox else 0.0
            ka_ref[0, :, kvh * LANES:(kvh + 1) * LANES] = jnp.where(low, val, spare).astype(jnp.bfloat16)


def _aug_constants():
    spare_rows = LANES - HEAD_DIM
    pselk = np.zeros((N_PIECES * LANES, PAD_K), np.float32)
    constk = np.zeros((1, PAD_K), np.float32)
    pselqt = np.zeros((N_HEADS * spare_rows, LANES), np.float32)
    constqt = np.zeros((N_HEADS * spare_rows, 1), np.float32)
    for h in range(N_HEADS):
        kvh, g = divmod(h, GROUP)
        for j in range(N_PIECES):
            pselk[j * LANES + h, kvh * LANES + AUG_K0 + N_PIECES * g + j] = -1.0
            pselqt[h * spare_rows + AUG_Q0 - HEAD_DIM + j, j * N_HEADS + h] = 1.0
            constqt[h * spare_rows + AUG_K0 - HEAD_DIM + N_PIECES * g + j, 0] = 1.0
    for kvh in range(N_KV_HEADS):
        for j in range(N_PIECES):
            constk[0, kvh * LANES + AUG_Q0 + j] = 1.0
    return (jnp.asarray(pselk, jnp.bfloat16), jnp.asarray(constk), jnp.asarray(pselqt, jnp.bfloat16),
            jnp.asarray(constqt))


def _full(shape):
    return pl.BlockSpec(shape, lambda *_: (0,) * len(shape))


def _proj_prompt(x, g, wallt, wk, fox_w=None):
    B, S, _ = x.shape
    tm = PROJ_TM
    fox = fox_w is not None
    ins = [x, g, wallt, wk]
    in_specs = [pl.BlockSpec((1, tm, D_MODEL), lambda b, i: (b, i, 0)), _full(g.shape), _full(wallt.shape),
                _full(wk.shape)]
    feat = lambda rows: pl.BlockSpec((1, rows, tm), lambda b, i: (b, 0, i))
    out_shape = [jax.ShapeDtypeStruct((B, PAD_Q, S), jnp.bfloat16),
                 jax.ShapeDtypeStruct((B, BRANCH, S), jnp.float32),
                 jax.ShapeDtypeStruct((B, KV_WIDTH, S), jnp.float32),
                 jax.ShapeDtypeStruct((B, KV_WIDTH, S), jnp.float32),
                 jax.ShapeDtypeStruct((B, S, PAD_K), jnp.bfloat16),
                 jax.ShapeDtypeStruct((B, KV_WIDTH, S), jnp.bfloat16)]
    out_specs = [feat(PAD_Q), feat(BRANCH), feat(KV_WIDTH), feat(KV_WIDTH),
                 pl.BlockSpec((1, tm, PAD_K), lambda b, i: (b, i, 0)), feat(KV_WIDTH)]
    scratch = []
    if fox:
        extra = list(fox_w) + list(_aug_constants())
        ins += extra
        in_specs += [_full(a.shape) for a in extra]
        out_shape.append(jax.ShapeDtypeStruct((B, N_HEADS, S), jnp.float32))
        out_specs.append(feat(N_HEADS))
        scratch = [pltpu.VMEM((1, LANES), jnp.float32), pltpu.VMEM((N_HEADS, 1), jnp.float32)]
    return pl.pallas_call(
        functools.partial(_proj_prompt_kernel, fox=fox, tm=tm),
        grid=(B, S // tm),
        in_specs=in_specs,
        out_specs=out_specs,
        out_shape=out_shape,
        scratch_shapes=scratch,
        compiler_params=pltpu.CompilerParams(dimension_semantics=("arbitrary", "arbitrary"),
                                             vmem_limit_bytes=VMEM_LIMIT),
        name="proj_prompt_fox" if fox else "proj_prompt_swa",
    )(*ins)


def _alibi_slopes():
    return [float(2.0 ** (-8.0 * (h + 1) / N_HEADS)) for h in range(N_HEADS)]


def _swa_bias_table():
    r = np.arange(2 * WINDOW)[:, None]
    c = np.arange(WINDOW)[None, :]
    dist = c - r + WINDOW
    valid = (dist >= 0) & (dist <= WINDOW)
    slopes = np.array(_alibi_slopes(), np.float32)[:, None, None]
    tab = np.where(valid[None], -(slopes * dist[None].astype(np.float32)) * np.float32(LOG2E), np.float32(NEG_INF))
    tab = tab.reshape(N_KV_HEADS, GROUP, 2 * WINDOW, WINDOW).transpose(0, 2, 1, 3)
    return jnp.asarray(tab.reshape(N_KV_HEADS, 2 * WINDOW, GROUP * WINDOW), jnp.float32)


def _swa_prompt_kernel(qt_ref, kp_ref, kc_ref, vp_ref, vc_ref, gatet_ref, bias_ref, sink_ref, o_ref):
    i = pl.program_id(1)
    no_prev = (_iota((2 * WINDOW, GROUP * WINDOW), 0) < WINDOW) & (i == 0)
    for jb in range(SWA_BLOCKS):
        cols = slice(jb * WINDOW, (jb + 1) * WINDOW)
        for kvh in range(N_KV_HEADS):
            lanes = slice(kvh * LANES, (kvh + 1) * LANES)
            rows = slice(kvh * HEAD_DIM, (kvh + 1) * HEAD_DIM)
            if jb == 0:
                k_prev, v_prev = kp_ref[0, :, lanes], vp_ref[0, rows, :]
            else:
                before = slice((jb - 1) * WINDOW, jb * WINDOW)
                k_prev, v_prev = kc_ref[0, before, lanes], vc_ref[0, rows, before]
            kk = jnp.concatenate([k_prev, kc_ref[0, cols, lanes]], axis=0)
            vv = jnp.concatenate([v_prev, vc_ref[0, rows, cols]], axis=1)
            q4 = jnp.concatenate([qt_ref[0, (kvh * GROUP + g) * LANES:(kvh * GROUP + g + 1) * LANES, cols]
                                  for g in range(GROUP)], axis=1)
            bias = bias_ref[kvh]
            if jb == 0:
                bias = jnp.where(no_prev, NEG_INF, bias)
            st = _dot(kk, q4) + bias
            sk = sink_ref[kvh]
            m = jnp.maximum(jnp.max(st, axis=0, keepdims=True), sk)
            p = jnp.exp2(st - m)
            den = jnp.sum(p, axis=0, keepdims=True) + jnp.exp2(sk - m)
            ot = _dot(vv, p.astype(jnp.bfloat16)) / den
            for g in range(GROUP):
                out_rows = slice((kvh * GROUP + g) * HEAD_DIM, (kvh * GROUP + g + 1) * HEAD_DIM)
                gt = gatet_ref[0, out_rows, cols]
                o_ref[0, out_rows, cols] = (ot[:, g * WINDOW:(g + 1) * WINDOW] * _silu(gt)).astype(jnp.bfloat16)


def _swa_prompt(sink, qt, ka, vt16, gatet):
    B, _, S = qt.shape
    t = SWA_BLOCKS * WINDOW
    bias = _swa_bias_table()
    sink_lanes = jnp.repeat((sink * LOG2E).reshape(N_KV_HEADS, GROUP, 1), WINDOW, axis=2).reshape(N_KV_HEADS, 1, GROUP * WINDOW)
    feat = lambda rows: pl.BlockSpec((1, rows, t), lambda b, i: (b, 0, i))
    prev_block = lambda i: jnp.maximum(i * SWA_BLOCKS - 1, 0)
    return pl.pallas_call(
        _swa_prompt_kernel,
        grid=(B, S // t),
        in_specs=[feat(PAD_Q),
                  pl.BlockSpec((1, WINDOW, PAD_K), lambda b, i: (b, prev_block(i), 0)),
                  pl.BlockSpec((1, t, PAD_K), lambda b, i: (b, i, 0)),
                  pl.BlockSpec((1, KV_WIDTH, WINDOW), lambda b, i: (b, 0, prev_block(i))),
                  feat(KV_WIDTH),
                  feat(BRANCH),
                  _full(bias.shape), _full(sink_lanes.shape)],
        out_specs=feat(BRANCH),
        out_shape=jax.ShapeDtypeStruct((B, BRANCH, S), jnp.bfloat16),
        compiler_params=pltpu.CompilerParams(dimension_semantics=("arbitrary", "arbitrary"),
                                             vmem_limit_bytes=VMEM_LIMIT),
        name="swa_prompt",
    )(qt, ka, ka, vt16, vt16, gatet, bias, sink_lanes)


def _fox_flash_kernel(qi_ref, kj_ref, qt_ref, k_ref, v_ref, gatet_ref, o_ref, m_ref, l_ref, acc_ref, *, t):
    n_heads = FLASH_KV_HEADS * GROUP
    step = pl.program_id(2)
    qi = qi_ref[step]
    kj = kj_ref[step]

    @pl.when(kj == 0)
    def _():
        m_ref[...] = jnp.full_like(m_ref, NEG_INF)
        l_ref[...] = jnp.zeros_like(l_ref)
        acc_ref[...] = jnp.zeros_like(acc_ref)

    def update(masked):
        ks = [k_ref[0, :, j * LANES:(j + 1) * LANES] for j in range(FLASH_KV_HEADS)]
        vs = [v_ref[0, j * HEAD_DIM:(j + 1) * HEAD_DIM, :] for j in range(FLASH_KV_HEADS)]
        if masked:
            keep = _iota((t, t), 0) <= _iota((t, t), 1)
        score = lambda g: _dot(ks[g // GROUP], qt_ref[0, g * LANES:(g + 1) * LANES, :])

        def softmax_part(g, st):
            if masked:
                st = jnp.where(keep, st, NEG_INF)
            m_prev = m_ref[g]
            m_new = jnp.maximum(m_prev, jnp.max(st, axis=0, keepdims=True))
            alpha = jnp.exp2(m_prev - m_new)
            p = jnp.exp2(st - m_new)
            l_ref[g] = alpha * l_ref[g] + jnp.sum(p, axis=0, keepdims=True)
            m_ref[g] = m_new
            return alpha, p.astype(jnp.bfloat16)

        def value_part(g, alpha, p):
            acc_ref[g] = alpha * acc_ref[g] + _dot(vs[g // GROUP], p)

        ahead = 2
        sts = [score(g) for g in range(ahead)]
        pending = None
        for g in range(n_heads):
            if g + ahead < n_heads:
                sts.append(score(g + ahead))
            cur = softmax_part(g, sts[g])
            sts[g] = None
            if pending is not None:
                value_part(g - 1, *pending)
            pending = cur
        value_part(n_heads - 1, *pending)

    @pl.when(kj < qi)
    def _():
        update(False)

    @pl.when(kj == qi)
    def _():
        update(True)
        for g in range(n_heads):
            o = acc_ref[g] / l_ref[g]
            gt = gatet_ref[0, g * HEAD_DIM:(g + 1) * HEAD_DIM, :]
            o_ref[0, g * HEAD_DIM:(g + 1) * HEAD_DIM, :] = (o * _silu(gt)).astype(jnp.bfloat16)


def _fox_flash(qt, ka, vt16, gatet):
    B, _, S = qt.shape
    t = FLASH_T
    n = S // t
    nkv = FLASH_KV_HEADS
    n_heads = nkv * GROUP
    pairs = [(qi, kj) for qi in range(n) for kj in range(qi + 1)]
    qi_arr = jnp.asarray(np.array([p[0] for p in pairs], np.int32))
    kj_arr = jnp.asarray(np.array([p[1] for p in pairs], np.int32))
    gw = n_heads * HEAD_DIM
    grid_spec = pltpu.PrefetchScalarGridSpec(
        num_scalar_prefetch=2,
        grid=(B, N_KV_HEADS // nkv, len(pairs)),
        in_specs=[pl.BlockSpec((1, n_heads * LANES, t), lambda b, h, s, qi, kj: (b, h, qi[s])),
                  pl.BlockSpec((1, t, nkv * LANES), lambda b, h, s, qi, kj: (b, kj[s], h)),
                  pl.BlockSpec((1, nkv * HEAD_DIM, t), lambda b, h, s, qi, kj: (b, h, kj[s])),
                  pl.BlockSpec((1, gw, t), lambda b, h, s, qi, kj: (b, h, qi[s]))],
        out_specs=pl.BlockSpec((1, gw, t), lambda b, h, s, qi, kj: (b, h, qi[s])),
        scratch_shapes=[pltpu.VMEM((n_heads, 1, t), jnp.float32), pltpu.VMEM((n_heads, 1, t), jnp.float32),
                        pltpu.VMEM((n_heads, HEAD_DIM, t), jnp.float32)])
    return pl.pallas_call(
        functools.partial(_fox_flash_kernel, t=t),
        grid_spec=grid_spec,
        out_shape=jax.ShapeDtypeStruct((B, BRANCH, S), jnp.bfloat16),
        compiler_params=pltpu.CompilerParams(dimension_semantics=("arbitrary", "arbitrary", "arbitrary"),
                                             vmem_limit_bytes=VMEM_LIMIT),
        name="fox_flash",
    )(qi_arr, kj_arr, qt, ka, vt16, gatet)


def _out_proj_kernel(og_ref, w_ref, g_ref, x_ref, o_ref, *, feature_major):
    if feature_major:
        y = lax.dot_general(og_ref[0], w_ref[...], _TN, preferred_element_type=jnp.float32)
        o_ref[0] = x_ref[0] + _rmsnorm_rows(y, g_ref[...])
    else:
        y = _dot(og_ref[...], w_ref[...])
        o_ref[...] = x_ref[...] + _rmsnorm_rows(y, g_ref[...])


def _out_proj_prompt(ogt, w, g, x):
    B, S, _ = x.shape
    tm = OUT_TM
    rows = pl.BlockSpec((1, tm, D_MODEL), lambda b, i: (b, i, 0))
    return pl.pallas_call(
        functools.partial(_out_proj_kernel, feature_major=True),
        grid=(B, S // tm),
        in_specs=[pl.BlockSpec((1, BRANCH, tm), lambda b, i: (b, 0, i)), _full(w.shape), _full(g.shape), rows],
        out_specs=rows,
        out_shape=jax.ShapeDtypeStruct(x.shape, jnp.float32),
        compiler_params=pltpu.CompilerParams(dimension_semantics=("arbitrary", "arbitrary"),
                                             vmem_limit_bytes=VMEM_LIMIT),
        name="out_proj_prompt",
    )(ogt, w, g, x)


def _out_proj_sample(og, w, g, x):
    return pl.pallas_call(
        functools.partial(_out_proj_kernel, feature_major=False),
        grid=(1,),
        in_specs=[_full(og.shape), _full(w.shape), _full(g.shape), _full(x.shape)],
        out_specs=_full(x.shape),
        out_shape=jax.ShapeDtypeStruct(x.shape, jnp.float32),
        compiler_params=pltpu.CompilerParams(dimension_semantics=("arbitrary",), vmem_limit_bytes=VMEM_LIMIT),
        name="out_proj_sample",
    )(og, w, g, x)


def _proj_sample_kernel(*refs, fox):
    if fox:
        x_ref, g_ref, wqg_ref, wkvt_ref, wft_ref, bft_ref, q_ref, gate_ref, kt_ref, vt_ref, logft_ref = refs
    else:
        x_ref, g_ref, wqg_ref, wkvt_ref, q_ref, gate_ref, kt_ref, vt_ref = refs
    hb = _rmsnorm_rows(x_ref[...], g_ref[...]).astype(jnp.bfloat16)
    pq = _dot(hb, wqg_ref[...])
    q_ref[...] = pq[:, :BRANCH] * ATT_SCALE
    gate_ref[...] = pq[:, BRANCH:]
    kvt = _dot_nt(wkvt_ref[...], hb)
    kt_ref[...] = kvt[:KV_WIDTH]
    vt_ref[...] = kvt[KV_WIDTH:]
    if fox:
        logft_ref[...] = _log_sigmoid(_dot_nt(wft_ref[...], hb) + bft_ref[...])


def _proj_sample(x, g, wqg, wkvt, fox_w=None):
    n = x.shape[0]
    fox = fox_w is not None
    ins = [x, g, wqg, wkvt] + (list(fox_w) if fox else [])
    out_shape = [jax.ShapeDtypeStruct((n, BRANCH), jnp.float32), jax.ShapeDtypeStruct((n, BRANCH), jnp.float32),
                 jax.ShapeDtypeStruct((KV_WIDTH, n), jnp.float32), jax.ShapeDtypeStruct((KV_WIDTH, n), jnp.float32)]
    if fox:
        out_shape.append(jax.ShapeDtypeStruct((N_HEADS, n), jnp.float32))
    return pl.pallas_call(
        functools.partial(_proj_sample_kernel, fox=fox),
        grid=(1,),
        in_specs=[_full(a.shape) for a in ins],
        out_specs=[_full(s.shape) for s in out_shape],
        out_shape=out_shape,
        compiler_params=pltpu.CompilerParams(dimension_semantics=("arbitrary",), vmem_limit_bytes=VMEM_LIMIT),
        name="proj_sample_fox" if fox else "proj_sample_swa",
    )(*ins)


def _head_placement():
    r = np.zeros((BRANCH, KV_WIDTH), np.float32)
    for h in range(N_HEADS):
        for d in range(HEAD_DIM):
            r[h * HEAD_DIM + d, (h // GROUP) * HEAD_DIM + d] = 1.0
    return jnp.asarray(r, jnp.bfloat16), jnp.asarray(r.T, jnp.bfloat16)


def _own_lanes_mask():
    return (_iota((N_HEADS, BRANCH), 1) // HEAD_DIM) == _iota((N_HEADS, BRANCH), 0)


def _block_diag_q(qrow, r):
    dq = jnp.where(_own_lanes_mask(), jnp.broadcast_to(qrow, (N_HEADS, BRANCH)), 0.0).astype(jnp.bfloat16)
    return _dot(dq, r).astype(jnp.bfloat16)


def _gated_row(o16, gate_row, rt):
    own_kv = (_iota((N_HEADS, KV_WIDTH), 1) // HEAD_DIM) == (_iota((N_HEADS, KV_WIDTH), 0) // GROUP)
    om = jnp.where(own_kv, o16, 0.0)
    e = jnp.zeros((N_HEADS, BRANCH), jnp.float32)
    for piece in _pieces(om):
        e = e + _dot(piece, rt)
    orow = jnp.sum(jnp.where(_own_lanes_mask(), e, 0.0), axis=0, keepdims=True)
    return (orow * _silu(gate_row)).astype(jnp.bfloat16)


def _pick_lane(x, b):
    return jnp.sum(jnp.where(_iota(x.shape, 1) == b, x, 0.0), axis=1, keepdims=True)


def _swa_sample_kernel(q_ref, gate_ref, kbuf_ref, vbuf_ref, ktn_ref, vtn_ref, sink_ref, slope_ref, r_ref, rt_ref,
                       o_ref, kout_ref, vout_ref):
    b = pl.program_id(0)
    qbd = _block_diag_q(q_ref[0], r_ref[...])
    kt = kbuf_ref[0]
    vt = vbuf_ref[0]
    ktn = ktn_ref[...]
    vtn = vtn_ref[...]
    lane = _iota((N_HEADS, WINDOW), 1)
    dist = (WINDOW - lane).astype(jnp.float32)
    s_old = _dot(qbd, kt.astype(jnp.bfloat16)) - slope_ref[...] * dist
    s_new = jnp.where(lane == b, _dot(qbd, ktn.astype(jnp.bfloat16)), NEG_INF)
    sk = sink_ref[...]
    m = jnp.maximum(jnp.maximum(jnp.max(s_old, axis=1, keepdims=True), jnp.max(s_new, axis=1, keepdims=True)), sk)
    p_old = jnp.exp(s_old - m)
    p_new = jnp.exp(s_new - m)
    den = jnp.sum(p_old, axis=1, keepdims=True) + jnp.sum(p_new, axis=1, keepdims=True) + jnp.exp(sk - m)
    o16 = (_dot_nt(p_old.astype(jnp.bfloat16), vt.astype(jnp.bfloat16))
           + _dot_nt(p_new.astype(jnp.bfloat16), vtn.astype(jnp.bfloat16))) / den
    o_ref[0] = _gated_row(o16, gate_ref[0], rt_ref[...])
    last = _iota((KV_WIDTH, WINDOW), 1) == WINDOW - 1
    kout_ref[0] = jnp.where(last, _pick_lane(ktn, b), pltpu.roll(kt, WINDOW - 1, 1))
    vout_ref[0] = jnp.where(last, _pick_lane(vtn, b), pltpu.roll(vt, WINDOW - 1, 1))


def _swa_sample(q, gate, kbuf, vbuf, ktn, vtn, sink, slope, r, rt):
    n = q.shape[0]
    row = pl.BlockSpec((1, 1, BRANCH), lambda b: (b, 0, 0))
    buf = pl.BlockSpec((1, KV_WIDTH, WINDOW), lambda b: (b, 0, 0))
    return pl.pallas_call(
        _swa_sample_kernel,
        grid=(n,),
        in_specs=[row, row, buf, buf, _full(ktn.shape), _full(vtn.shape), _full(sink.shape), _full(slope.shape),
                  _full(r.shape), _full(rt.shape)],
        out_specs=[row, buf, buf],
        out_shape=[jax.ShapeDtypeStruct((n, 1, BRANCH), jnp.bfloat16),
                   jax.ShapeDtypeStruct(kbuf.shape, jnp.float32), jax.ShapeDtypeStruct(vbuf.shape, jnp.float32)],
        compiler_params=pltpu.CompilerParams(dimension_semantics=("arbitrary",), vmem_limit_bytes=VMEM_LIMIT),
        name="swa_sample",
    )(q, gate, kbuf, vbuf, ktn, vtn, sink, slope, r, rt)


def _fox_sample_kernel(pt_ref, q_ref, gate_ref, ktn_ref, vtn_ref, lfn_ref, r_ref, rt_ref, ck_hbm, cv_hbm, clf_hbm,
                       o_ref, kbuf, vbuf, lfbuf, sem, *, layer, n_pages, n_batch):
    npg = PAGES_PER_CHUNK
    n_chunks = n_pages // npg
    b = pl.program_id(0)

    def chunk_copies(row, chunk, slot):
        out = []
        for i in range(npg):
            page = pt_ref[row, n_pages - 1 - (chunk * npg + i)]
            out.append((pltpu.make_async_copy(ck_hbm.at[layer, page], kbuf.at[slot, i], sem.at[0, slot]),
                        pltpu.make_async_copy(cv_hbm.at[layer, page], vbuf.at[slot, i], sem.at[1, slot]),
                        pltpu.make_async_copy(clf_hbm.at[layer, page], lfbuf.at[slot, i], sem.at[2, slot])))
        return out

    def start_chunk(row, chunk, slot):
        for i, page_copies in enumerate(chunk_copies(row, chunk, slot)):
            for cp in page_copies:
                cp.start(priority=i % 2)

    @pl.when(b == 0)
    def _():
        start_chunk(b, 0, 0)

    qbd = _block_diag_q(q_ref[0], r_ref[...])
    lane = _iota((N_HEADS, LANES), 1)
    s = jnp.where(lane == b, _dot(qbd, ktn_ref[...].astype(jnp.bfloat16)), NEG_INF)
    m = jnp.max(s, axis=1, keepdims=True)
    p = jnp.exp(s - m)
    l = jnp.sum(p, axis=1, keepdims=True)
    acc = _dot_nt(p.astype(jnp.bfloat16), vtn_ref[...].astype(jnp.bfloat16))
    carry = jnp.broadcast_to(_pick_lane(lfn_ref[...], b), (N_HEADS, LANES))

    later = (_iota((PAGE, PAGE), 0) > _iota((PAGE, PAGE), 1)).astype(jnp.bfloat16)
    ones = jnp.ones((PAGE, PAGE), jnp.bfloat16)
    for chunk in range(n_chunks):
        slot = chunk % 2
        if chunk + 1 < n_chunks:
            start_chunk(b, chunk + 1, 1 - slot)
        else:
            @pl.when(b + 1 < n_batch)
            def _():
                start_chunk(b + 1, 0, 1 - slot)
        arriving = chunk_copies(b, chunk, slot)
        for k_copy, _, lf_copy in arriving:
            lf_copy.wait()
            k_copy.wait()

        lf = lfbuf[slot].reshape(npg * N_HEADS, PAGE)
        within = jnp.zeros_like(lf)
        total = jnp.zeros_like(lf)
        for piece in _pieces(lf):
            within = within + _dot(piece, later)
            total = total + _dot(piece, ones)
        scores = []
        for i in range(npg):
            rows = slice(i * N_HEADS, (i + 1) * N_HEADS)
            scores.append(_dot(qbd, kbuf[slot, i].astype(jnp.bfloat16)) + (within[rows] + carry))
            carry = carry + total[rows]
        m_cur = scores[0]
        for s in scores[1:]:
            m_cur = jnp.maximum(m_cur, s)
        m_new = jnp.maximum(m, jnp.max(m_cur, axis=1, keepdims=True))
        alpha = jnp.exp(m - m_new)
        for _, v_copy, _ in arriving:
            v_copy.wait()
        parts = [alpha * acc, jnp.zeros_like(acc)]
        psum = jnp.zeros((N_HEADS, PAGE), jnp.float32)
        for i in range(npg):
            p = jnp.exp(scores[i] - m_new)
            psum = psum + p
            parts[i % 2] = parts[i % 2] + _dot_nt(p.astype(jnp.bfloat16), vbuf[slot, i].astype(jnp.bfloat16))
        acc = parts[0] + parts[1]
        l = alpha * l + jnp.sum(psum, axis=1, keepdims=True)
        m = m_new

    o_ref[0] = _gated_row(acc / l, gate_ref[0], rt_ref[...])


def _fox_sample(page_table, q, gate, ktn, vtn, lfn, r, rt, cache_kt, cache_vt, cache_lft, layer):
    n, n_pages = page_table.shape
    npg = PAGES_PER_CHUNK
    assert n_pages % (2 * npg) == 0
    row = pl.BlockSpec((1, 1, BRANCH), lambda b, pt: (b, 0, 0))
    const = lambda a: pl.BlockSpec(a.shape, lambda b, pt: (0,) * a.ndim)
    hbm = pl.BlockSpec(memory_space=pl.ANY)
    grid_spec = pltpu.PrefetchScalarGridSpec(
        num_scalar_prefetch=1,
        grid=(n,),
        in_specs=[row, row, const(ktn), const(vtn), const(lfn), const(r), const(rt), hbm, hbm, hbm],
        out_specs=row,
        scratch_shapes=[pltpu.VMEM((2, npg, KV_WIDTH, PAGE), jnp.float32),
                        pltpu.VMEM((2, npg, KV_WIDTH, PAGE), jnp.float32),
                        pltpu.VMEM((2, npg, N_HEADS, PAGE), jnp.float32),
                        pltpu.SemaphoreType.DMA((3, 2))])
    return pl.pallas_call(
        functools.partial(_fox_sample_kernel, layer=layer, n_pages=n_pages, n_batch=n),
        grid_spec=grid_spec,
        out_shape=jax.ShapeDtypeStruct((n, 1, BRANCH), jnp.bfloat16),
        compiler_params=pltpu.CompilerParams(dimension_semantics=("arbitrary",), vmem_limit_bytes=VMEM_LIMIT),
        name="fox_sample",
    )(page_table, q, gate, ktn, vtn, lfn, r, rt, cache_kt, cache_vt, cache_lft)


def _feature_major(t):
    lead = t.shape[:-3]
    n = len(lead)
    t = jnp.transpose(t, tuple(range(n)) + (n + 1, n + 2, n))
    return t.reshape(*lead, KV_WIDTH, t.shape[-1])


def _row_major(t):
    lead = t.shape[:-2]
    n = len(lead)
    t = t.reshape(*lead, N_KV_HEADS, HEAD_DIM, t.shape[-1])
    return jnp.transpose(t, tuple(range(n)) + (n + 2, n, n + 1))


def kernel(x_prompt, x_sample, state_swa_k, state_swa_v, cache_fox_k, cache_fox_v, cache_fox_logf, page_table,
           norm_pre, norm_post, w_in_swa, sinks_swa, w_out_swa, w_in_fox, b_forget, w_out_fox):
    B, S, _ = x_prompt.shape
    n_dec = x_sample.shape[0]
    depth = norm_pre.shape[0]
    bf16 = jnp.bfloat16
    r, rt = _head_placement()
    slope_col = jnp.asarray(np.array(_alibi_slopes(), np.float32).reshape(N_HEADS, 1))
    cache_kt = _feature_major(cache_fox_k)
    cache_vt = _feature_major(cache_fox_v)
    cache_lft = jnp.swapaxes(cache_fox_logf, -1, -2)

    xp = x_prompt
    xs = x_sample.reshape(n_dec, D_MODEL)
    swa_kp, swa_vp, swa_ks, swa_vs = [], [], [], []
    fox_kp, fox_vp, fox_fp, fox_ks, fox_vs, fox_fs = [], [], [], [], [], []
    for i in range(depth):
        j = i // 2
        fox = i % 2 == 1
        w_in = w_in_fox[j] if fox else w_in_swa[j]
        w_out = (w_out_fox[j] if fox else w_out_swa[j]).astype(bf16)
        n_main = 2 * BRANCH + 2 * KV_WIDTH
        wqg = jnp.concatenate([w_in[:, :BRANCH], w_in[:, BRANCH + 2 * KV_WIDTH:n_main]], axis=1).astype(bf16)
        wallt = w_in[:, :n_main].T.astype(bf16)
        wkvt = wallt[BRANCH:BRANCH + 2 * KV_WIDTH]
        wk = w_in[:, BRANCH:BRANCH + KV_WIDTH].astype(bf16)
        g_pre = norm_pre[i].reshape(1, D_MODEL)
        g_post = norm_post[i].reshape(1, D_MODEL)
        if fox:
            wf = w_in[:, n_main:]
            wf_pad = jnp.pad(wf, ((0, 0), (0, LANES - N_HEADS))).astype(bf16)
            wft = wf.T.astype(bf16)
            bf_row = jnp.pad(b_forget[j], (0, LANES - N_HEADS)).reshape(1, LANES)
            bf_col = b_forget[j].reshape(N_HEADS, 1)
            qt, gatet, kt, vt, ka, vt16, logft = _proj_prompt(xp, g_pre, wallt, wk, (wf_pad, wft, bf_row, bf_col))
            ogt = _fox_flash(qt, ka, vt16, gatet)
            qs, gs, ktn, vtn, lfn = _proj_sample(xs, g_pre, wqg, wkvt, (wft, bf_col))
            ogs = _fox_sample(page_table, qs.reshape(n_dec, 1, BRANCH), gs.reshape(n_dec, 1, BRANCH), ktn, vtn, lfn,
                              r, rt, cache_kt, cache_vt, cache_lft, j)
            fox_kp.append(_row_major(kt)); fox_vp.append(_row_major(vt)); fox_fp.append(jnp.swapaxes(logft, 1, 2))
            fox_ks.append(_row_major(ktn[None])[0][:, None]); fox_vs.append(_row_major(vtn[None])[0][:, None])
            fox_fs.append(lfn.T[:, None, :])
        else:
            qt, gatet, kt, vt, ka, vt16 = _proj_prompt(xp, g_pre, wallt, wk)
            ogt = _swa_prompt(sinks_swa[j], qt, ka, vt16, gatet)
            qs, gs, ktn, vtn = _proj_sample(xs, g_pre, wqg, wkvt)
            ogs, kout, vout = _swa_sample(qs.reshape(n_dec, 1, BRANCH), gs.reshape(n_dec, 1, BRANCH),
                                          _feature_major(state_swa_k[j]), _feature_major(state_swa_v[j]),
                                          ktn, vtn, sinks_swa[j].reshape(N_HEADS, 1), slope_col, r, rt)
            swa_kp.append(_row_major(kt[:, :, S - WINDOW:])); swa_vp.append(_row_major(vt[:, :, S - WINDOW:]))
            swa_ks.append(_row_major(kout)); swa_vs.append(_row_major(vout))
        xp = _out_proj_prompt(ogt, w_out, g_post, xp)
        xs = _out_proj_sample(ogs.reshape(n_dec, BRANCH), w_out, g_post, xs)
    return (xp, xs.reshape(n_dec, 1, D_MODEL),
            jnp.stack(swa_kp), jnp.stack(swa_vp), jnp.stack(swa_ks), jnp.stack(swa_vs),
            jnp.stack(fox_kp), jnp.stack(fox_vp), jnp.stack(fox_fp),
            jnp.stack(fox_ks), jnp.stack(fox_vs), jnp.stack(fox_fs))
```
